```python
import jax, jax.numpy as jnp
from jax import lax
import numpy as np

D_MODEL = 1024
BATCH = 4
SEQ = 8192
DEPTH = 1
DEC_BATCH = 32
DEC_SEQ = 64
PAST_LEN = 2048

CHUNK = 64
N_META = 16
D_CONV = D_MODEL // 2
CONV_WIDTH = 3
RET_HEADS = 4
RET_HEAD_DIM = 128
D_RET = RET_HEADS * RET_HEAD_DIM
D_MIX = D_CONV + D_RET
D_FF = 4 * D_MODEL
ROPE_BASE = 10000.0
NORM_EPS = 1e-6
GN_EPS = 1e-5
D_IN = 3 * D_CONV + 4 * D_RET
IN_SPLITS = (D_CONV, 2 * D_CONV, 3 * D_CONV, 3 * D_CONV + D_RET, 3 * D_CONV + 2 * D_RET, 3 * D_CONV + 3 * D_RET)

kernel_name = "hymba_shortconv_retnet_stream_step"


def rmsnorm(x, w):
    xf = x.astype(jnp.float32)
    y = xf * lax.rsqrt(jnp.mean(xf * xf, axis=-1, keepdims=True) + NORM_EPS)
    return (y * w.astype(jnp.float32)).astype(x.dtype)


def rotary(x, pos):
    half = RET_HEAD_DIM // 2
    inv_freq = ROPE_BASE ** (-jnp.arange(half, dtype=jnp.float32) / half)
    ang = pos[:, None] * inv_freq[None, :]
    cos = jnp.cos(ang)[None, :, None, :]
    sin = jnp.sin(ang)[None, :, None, :]
    x1, x2 = x[..., :half], x[..., half:]
    return jnp.concatenate([x1 * cos - x2 * sin, x1 * sin + x2 * cos], axis=-1)


def retention_log_gamma():
    return jnp.log(1.0 - 2.0 ** (-5.0 - jnp.arange(RET_HEADS, dtype=jnp.float32)))


def retention_block(q, k, v, s, log_gamma):
    L = q.shape[1]
    idx = jnp.arange(L, dtype=jnp.float32)
    diff = idx[:, None] - idx[None, :]
    decay = jnp.where(diff[None] >= 0,
                      jnp.exp(jnp.maximum(diff, 0.0)[None] * log_gamma[:, None, None]), 0.0)
    scores = jnp.einsum("bihd,bjhd->bhij", q, k) * decay[None]
    inner = jnp.einsum("bhij,bjhe->bihe", scores, v)
    cross_decay = jnp.exp((idx + 1.0)[:, None] * log_gamma[None, :])
    cross = jnp.einsum("bihd,bhde->bihe", q, s) * cross_decay[None, :, :, None]
    k_decay = jnp.exp((L - 1.0 - idx)[:, None] * log_gamma[None, :])
    s_new = jnp.exp(L * log_gamma)[None, :, None, None] * s + jnp.einsum(
        "bjhd,bjhe,jh->bhde", k, v, k_decay)
    return inner + cross, s_new


def retention_seq(q, k, v, s, lead):
    log_gamma = retention_log_gamma()
    outs = []
    if lead > 0:
        o, s = retention_block(q[:, :lead], k[:, :lead], v[:, :lead], s, log_gamma)
        outs.append(o)
        q, k, v = q[:, lead:], k[:, lead:], v[:, lead:]
    bsz, T = q.shape[0], q.shape[1]
    if T <= CHUNK:
        o, s = retention_block(q, k, v, s, log_gamma)
        outs.append(o)
    else:
        nc = T // CHUNK

        def to_chunks(a):
            return a.reshape(bsz, nc, CHUNK, RET_HEADS, RET_HEAD_DIM).transpose(1, 0, 2, 3, 4)

        def step(state, qkv):
            qc, kc, vc = qkv
            oc, state = retention_block(qc, kc, vc, state, log_gamma)
            return state, oc

        s, o = lax.scan(step, s, (to_chunks(q), to_chunks(k), to_chunks(v)))
        outs.append(o.transpose(1, 0, 2, 3, 4).reshape(bsz, T, RET_HEADS, RET_HEAD_DIM))
    o = jnp.concatenate(outs, axis=1) if len(outs) > 1 else outs[0]
    return o, s


def hybrid_layer(h, pos, conv_hist, ret_state, lead,
                 norm_mix_w, w_in, conv_w, ret_gn_w, w_out, norm_ffn_w, w_up, w_down):
    bsz, T, _ = h.shape
    hn = rmsnorm(h, norm_mix_w)
    proj = hn @ w_in
    xt, b_gate, c_gate, q, k, v, g = jnp.split(proj, IN_SPLITS, axis=-1)
    ucat = jnp.concatenate([conv_hist.astype(h.dtype), c_gate * xt], axis=1)
    conv_y = conv_w[0] * ucat[:, :T] + conv_w[1] * ucat[:, 1:T + 1] + conv_w[2] * ucat[:, 2:]
    conv_out = b_gate * conv_y
    new_hist = ucat[:, T:]
    def heads(a):
        return a.astype(jnp.float32).reshape(bsz, T, RET_HEADS, RET_HEAD_DIM)
    qh = rotary(heads(q), pos)
    kh = rotary(heads(k), pos) * (RET_HEAD_DIM ** -0.5)
    vh = heads(v)
    o, new_state = retention_seq(qh, kh, vh, ret_state.astype(jnp.float32), lead)
    mu = jnp.mean(o, axis=-1, keepdims=True)
    var = jnp.mean(jnp.square(o - mu), axis=-1, keepdims=True)
    o = ((o - mu) * lax.rsqrt(var + GN_EPS)).reshape(bsz, T, D_RET) * ret_gn_w.astype(jnp.float32)
    ret_out = (jax.nn.silu(g.astype(jnp.float32)) * o).astype(h.dtype)
    h = h + jnp.concatenate([conv_out, ret_out], axis=-1) @ w_out
    f = jnp.square(jax.nn.relu(rmsnorm(h, norm_ffn_w) @ w_up)) @ w_down
    return h + f, new_hist, new_state.astype(h.dtype)


def setup_inputs(seed: int = 0) -> dict:
    key = jax.random.key(seed)
    ks = jax.random.split(key, 14)
    f32 = jnp.float32
    return {
        "x_prompt": jax.random.normal(ks[0], (BATCH, SEQ, D_MODEL), f32),
        "x_sample": jax.random.normal(ks[1], (DEC_BATCH, DEC_SEQ, D_MODEL), f32),
        "state_conv": jax.random.normal(ks[2], (DEPTH, DEC_BATCH, CONV_WIDTH - 1, D_CONV), f32),
        "state_ret": 0.5 * jax.random.normal(ks[3], (DEPTH, DEC_BATCH, RET_HEADS, RET_HEAD_DIM, RET_HEAD_DIM), f32),
        "meta_tokens": jax.random.normal(ks[4], (N_META, D_MODEL), f32),
        "norm_mix_w": 1.0 + 0.02 * jax.random.normal(ks[5], (DEPTH, D_MODEL), f32),
        "w_in": jax.random.normal(ks[6], (DEPTH, D_MODEL, D_IN), f32) * D_MODEL ** -0.5,
        "conv_w": jax.random.normal(ks[7], (DEPTH, CONV_WIDTH, D_CONV), f32) * CONV_WIDTH ** -0.5,
        "ret_gn_w": 1.0 + 0.02 * jax.random.normal(ks[8], (DEPTH, D_RET), f32),
        "w_out": jax.random.normal(ks[9], (DEPTH, D_MIX, D_MODEL), f32) * D_MIX ** -0.5,
        "norm_ffn_w": 1.0 + 0.02 * jax.random.normal(ks[10], (DEPTH, D_MODEL), f32),
        "w_up": jax.random.normal(ks[11], (DEPTH, D_MODEL, D_FF), f32) * D_MODEL ** -0.5,
        "w_down": jax.random.normal(ks[12], (DEPTH, D_FF, D_MODEL), f32) * D_FF ** -0.5,
        "norm_final_w": 1.0 + 0.02 * jax.random.normal(ks[13], (D_MODEL,), f32),
    }


def reference(x_prompt, x_sample, state_conv, state_ret, meta_tokens, norm_mix_w, w_in, conv_w,
              ret_gn_w, w_out, norm_ffn_w, w_up, w_down, norm_final_w):
    bp = x_prompt.shape[0]
    meta = jnp.broadcast_to(meta_tokens[None].astype(x_prompt.dtype), (bp, N_META, D_MODEL))
    hp = jnp.concatenate([meta, x_prompt], axis=1)
    pos_p = jnp.arange(hp.shape[1], dtype=jnp.float32) - float(N_META)
    hs = x_sample
    pos_s = jnp.arange(x_sample.shape[1], dtype=jnp.float32) + float(PAST_LEN)
    zero_hist = jnp.zeros((bp, CONV_WIDTH - 1, D_CONV), x_prompt.dtype)
    zero_ret = jnp.zeros((bp, RET_HEADS, RET_HEAD_DIM, RET_HEAD_DIM), jnp.float32)
    conv_p, ret_p, conv_s, ret_s = [], [], [], []
    for l in range(DEPTH):
        params = (norm_mix_w[l], w_in[l], conv_w[l], ret_gn_w[l], w_out[l], norm_ffn_w[l], w_up[l], w_down[l])
        hp, cp, rp = hybrid_layer(hp, pos_p, zero_hist, zero_ret, N_META, *params)
        hs, cs, rs = hybrid_layer(hs, pos_s, state_conv[l], state_ret[l], 0, *params)
        conv_p.append(cp)
        ret_p.append(rp)
        conv_s.append(cs)
        ret_s.append(rs)
    y_prompt = rmsnorm(hp[:, N_META:], norm_final_w)
    y_sample = rmsnorm(hs, norm_final_w)
    return (y_prompt, y_sample, jnp.stack(conv_p), jnp.stack(ret_p), jnp.stack(conv_s), jnp.stack(ret_s))
```

```python
import functools
import math

import jax
import jax.numpy as jnp
from jax import lax
from jax.experimental import pallas as pl
from jax.experimental.pallas import tpu as pltpu

D_MODEL = 1024
N_META = 16
PAST_LEN = 2048
D_CONV = 512
CONV_WIDTH = 3
RET_HEADS = 4
RET_HEAD_DIM = 128
D_RET = RET_HEADS * RET_HEAD_DIM
D_FF = 4 * D_MODEL
ROPE_BASE = 10000.0
NORM_EPS = 1e-6
GN_EPS = 1e-5
K_SCALE = RET_HEAD_DIM ** -0.5
LOG_GAMMA = tuple(math.log(1.0 - 2.0 ** (-5.0 - h)) for h in range(RET_HEADS))

SUBLANES = 8
VMEM_LIMIT_BYTES = 58 * 1024 * 1024

PROMPT_TILE = 512
PROMPT_CHUNK = 256
SAMPLE_STREAMS = 8
FF_BLOCK = 1024

BF16 = jnp.bfloat16
F32 = jnp.float32


def _dot(a, b):
    return jnp.dot(a, b, preferred_element_type=F32)


def _rmsnorm(x, w):
    ms = jnp.mean(x * x, axis=-1, keepdims=True)
    return x * lax.rsqrt(ms + NORM_EPS) * w


def _rotate(x, cos_t, sin_t):
    return x * cos_t + pltpu.roll(x, RET_HEAD_DIM // 2, 1) * sin_t


def _decay_tables(chunk):
    row = lax.broadcasted_iota(jnp.int32, (chunk, chunk), 0)
    col = lax.broadcasted_iota(jnp.int32, (chunk, chunk), 1)
    diff = (row - col).astype(F32)
    rowv = lax.broadcasted_iota(jnp.int32, (chunk, RET_HEAD_DIM), 0).astype(F32)
    tabs = []
    for lg in LOG_GAMMA:
        inner = jnp.where(diff >= 0.0, jnp.exp(jnp.maximum(diff, 0.0) * lg), 0.0) * K_SCALE
        cross = jnp.exp((rowv + 1.0) * lg)
        kdec = jnp.exp((chunk - 1.0 - rowv) * lg) * K_SCALE
        tabs.append((inner, cross, kdec, math.exp(chunk * lg)))
    return tabs


def _mixer_inputs(x, norm_w, w_in_ref, cos_t, sin_t):
    hn = _rmsnorm(x, norm_w).astype(BF16)
    pc = _dot(hn, w_in_ref[:, : 3 * D_CONV])
    xt = pc[:, :D_CONV]
    b_gate = pc[:, D_CONV: 2 * D_CONV]
    c_gate = pc[:, 2 * D_CONV:]
    u = c_gate * xt
    pr = _dot(hn, w_in_ref[:, 3 * D_CONV:])
    q, k, v, g = (pr[:, i * D_RET:(i + 1) * D_RET] for i in range(4))
    qh, kh, vh = [], [], []
    for h in range(RET_HEADS):
        sl = slice(h * RET_HEAD_DIM, (h + 1) * RET_HEAD_DIM)
        qh.append(_rotate(q[:, sl], cos_t, sin_t).astype(BF16))
        kh.append(_rotate(k[:, sl], cos_t, sin_t))
        vh.append(v[:, sl].astype(BF16))
    return u, b_gate, qh, kh, vh, g


def _retention_chunk(qc, kc, vc, s, tabs):
    inner_decay, cross_decay, k_decay, gamma_l = tabs
    scores = lax.dot_general(qc, kc.astype(BF16), (((1,), (1,)), ((), ())),
                             preferred_element_type=F32)
    p = (scores * inner_decay).astype(BF16)
    o = _dot(p, vc) + _dot(qc, s.astype(BF16)) * cross_decay
    kd = (kc * k_decay).astype(BF16)
    s_new = gamma_l * s + lax.dot_general(kd, vc, (((0,), (0,)), ((), ())),
                                          preferred_element_type=F32)
    return o, s_new


def _retention(qh, kh, vh, states, n_streams, seg, chunk, tabs):
    new_states = []
    rows = []
    for n in range(n_streams):
        st = list(states[n])
        chunk_rows = []
        for c in range(seg // chunk):
            lo = n * seg + c * chunk
            heads = []
            for h in range(RET_HEADS):
                o, st[h] = _retention_chunk(qh[h][lo:lo + chunk], kh[h][lo:lo + chunk],
                                            vh[h][lo:lo + chunk], st[h], tabs[h])
                heads.append(o)
            chunk_rows.append(heads)
        rows.extend(chunk_rows)
        new_states.append(st)
    o_heads = [jnp.concatenate([r[h] for r in rows], axis=0) if len(rows) > 1 else rows[0][h]
               for h in range(RET_HEADS)]
    return o_heads, new_states


def _group_norm_gate(o_heads, g, gn_w):
    outs = []
    for h in range(RET_HEADS):
        o = o_heads[h]
        mu = jnp.mean(o, axis=-1, keepdims=True)
        d = o - mu
        var = jnp.mean(d * d, axis=-1, keepdims=True)
        sl = slice(h * RET_HEAD_DIM, (h + 1) * RET_HEAD_DIM)
        gh = g[:, sl]
        outs.append((gh * jax.nn.sigmoid(gh)) * (d * lax.rsqrt(var + GN_EPS) * gn_w[:, sl]))
    return jnp.concatenate(outs, axis=1).astype(BF16)


def _ffn_and_final(h1, norm_ffn_w, w_up_ref, w_down_ref, norm_final_w):
    n2 = _rmsnorm(h1, norm_ffn_w).astype(BF16)
    f = None
    for j in range(D_FF // FF_BLOCK):
        sl = slice(j * FF_BLOCK, (j + 1) * FF_BLOCK)
        up = jnp.maximum(_dot(n2, w_up_ref[:, sl]), 0.0)
        d = _dot((up * up).astype(BF16), w_down_ref[sl, :])
        f = d if f is None else f + d
    return _rmsnorm(h1 + f, norm_final_w)


def _conv_group(u, b_gate, conv_w, ubuf_ref, n_streams, seg):
    ubuf_ref[:, SUBLANES:SUBLANES + seg, :] = u.reshape(n_streams, seg, D_CONV)
    u1 = ubuf_ref[:, SUBLANES - 1:SUBLANES - 1 + seg, :].reshape(n_streams * seg, D_CONV)
    u2 = ubuf_ref[:, SUBLANES - 2:SUBLANES - 2 + seg, :].reshape(n_streams * seg, D_CONV)
    conv_y = conv_w[0:1, :] * u2 + conv_w[1:2, :] * u1 + conv_w[2:3, :] * u
    new_hist = ubuf_ref[:, SUBLANES + seg - 2:SUBLANES + seg, :]
    return (b_gate * conv_y).astype(BF16), new_hist


def _layer(x, hist_ref_init, states, n_streams, seg, chunk, ubuf_ref, cos_t, sin_t,
           norm_mix_w, w_in_ref, conv_w, gn_w, w_out_ref, norm_ffn_w, w_up_ref, w_down_ref,
           norm_final_w):
    u, b_gate, qh, kh, vh, g = _mixer_inputs(x, norm_mix_w, w_in_ref, cos_t, sin_t)
    if hist_ref_init is not None:
        ubuf_ref[:, SUBLANES - 2:SUBLANES, :] = hist_ref_init
    conv_out, new_hist = _conv_group(u, b_gate, conv_w, ubuf_ref, n_streams, seg)
    tabs = _decay_tables(chunk)
    o_heads, new_states = _retention(qh, kh, vh, states, n_streams, seg, chunk, tabs)
    ret_out = _group_norm_gate(o_heads, g, gn_w)
    h1 = x + (_dot(conv_out, w_out_ref[:D_CONV, :]) + _dot(ret_out, w_out_ref[D_CONV:, :]))
    y = _ffn_and_final(h1, norm_ffn_w, w_up_ref, w_down_ref, norm_final_w)
    return y, new_hist, new_states


def _meta_kernel(meta_ref, cos_ref, sin_ref, norm_mix_ref, w_in_ref, hist_out_ref, state_out_ref):
    u, _, qh, kh, vh, _ = _mixer_inputs(meta_ref[...], norm_mix_ref[...], w_in_ref,
                                        cos_ref[...], sin_ref[...])
    hist_out_ref[...] = u[N_META - 2:N_META, :]
    tabs = _decay_tables(N_META)
    for h in range(RET_HEADS):
        _, _, k_decay, _ = tabs[h]
        kd = (kh[h] * k_decay).astype(BF16)
        state_out_ref[h] = lax.dot_general(kd, vh[h], (((0,), (0,)), ((), ())),
                                           preferred_element_type=F32)


def _prompt_kernel(x_ref, cos_ref, sin_ref, hist0_ref, state0_ref,
                   norm_mix_ref, w_in_ref, conv_w_ref, gn_w_ref, w_out_ref, norm_ffn_ref,
                   w_up_ref, w_down_ref, norm_final_ref,
                   y_ref, hist_out_ref, state_out_ref, ubuf_ref, state_ref):
    @pl.when(pl.program_id(1) == 0)
    def _():
        ubuf_ref[0, SUBLANES - 2:SUBLANES, :] = hist0_ref[...]
        state_ref[...] = state0_ref[...]

    states = [[state_ref[h] for h in range(RET_HEADS)]]
    y, new_hist, new_states = _layer(
        x_ref[0], None, states, 1, PROMPT_TILE, PROMPT_CHUNK, ubuf_ref,
        cos_ref[...], sin_ref[...], norm_mix_ref[...], w_in_ref, conv_w_ref[...],
        gn_w_ref[...], w_out_ref, norm_ffn_ref[...], w_up_ref, w_down_ref, norm_final_ref[...])
    y_ref[0] = y
    ubuf_ref[:, SUBLANES - 2:SUBLANES, :] = new_hist
    hist_out_ref[...] = new_hist
    for h in range(RET_HEADS):
        state_ref[h] = new_states[0][h]
        state_out_ref[0, h] = new_states[0][h]


def _sample_kernel(x_ref, cos_ref, sin_ref, hist_ref, state_in_ref,
                   norm_mix_ref, w_in_ref, conv_w_ref, gn_w_ref, w_out_ref, norm_ffn_ref,
                   w_up_ref, w_down_ref, norm_final_ref,
                   y_ref, hist_out_ref, state_out_ref, ubuf_ref):
    n, seg = SAMPLE_STREAMS, x_ref.shape[1]
    states = [[state_in_ref[i, h] for h in range(RET_HEADS)] for i in range(n)]
    cos_t = jnp.concatenate([cos_ref[...]] * n, axis=0)
    sin_t = jnp.concatenate([sin_ref[...]] * n, axis=0)
    y, new_hist, new_states = _layer(
        x_ref[...].reshape(n * seg, D_MODEL), hist_ref[...], states, n, seg, seg, ubuf_ref,
        cos_t, sin_t, norm_mix_ref[...], w_in_ref, conv_w_ref[...],
        gn_w_ref[...], w_out_ref, norm_ffn_ref[...], w_up_ref, w_down_ref, norm_final_ref[...])
    y_ref[...] = y.reshape(n, seg, D_MODEL)
    hist_out_ref[...] = new_hist
    for i in range(n):
        for h in range(RET_HEADS):
            state_out_ref[i, h] = new_states[i][h]


def _rope_tables(pos):
    half = RET_HEAD_DIM // 2
    inv_freq = ROPE_BASE ** (-jnp.arange(half, dtype=F32) / half)
    ang = pos[:, None] * inv_freq[None, :]
    cos, sin = jnp.cos(ang), jnp.sin(ang)
    return jnp.concatenate([cos, cos], axis=-1), jnp.concatenate([-sin, sin], axis=-1)


def _const_spec(shape):
    nd = len(shape)
    return pl.BlockSpec(shape, lambda *_: (0,) * nd, pipeline_mode=pl.Buffered(1))


def kernel(x_prompt, x_sample, state_conv, state_ret, meta_tokens, norm_mix_w, w_in, conv_w,
           ret_gn_w, w_out, norm_ffn_w, w_up, w_down, norm_final_w):
    depth = w_in.shape[0]
    assert depth == 1, "single-layer step"
    batch, seq, _ = x_prompt.shape
    dec_batch, dec_seq, _ = x_sample.shape
    assert seq % PROMPT_TILE == 0 and dec_batch % SAMPLE_STREAMS == 0

    w_in_b, w_out_b = w_in[0].astype(BF16), w_out[0].astype(BF16)
    w_up_b, w_down_b = w_up[0].astype(BF16), w_down[0].astype(BF16)
    norm_mix, norm_ffn = norm_mix_w[0][None], norm_ffn_w[0][None]
    norm_final = norm_final_w[None]
    gn_w, cw = ret_gn_w[0][None], conv_w[0]

    cos_m, sin_m = _rope_tables(jnp.arange(N_META, dtype=F32) - float(N_META))
    cos_p, sin_p = _rope_tables(jnp.arange(seq, dtype=F32))
    cos_s, sin_s = _rope_tables(jnp.arange(dec_seq, dtype=F32) + float(PAST_LEN))

    params = (norm_mix, w_in_b, cw, gn_w, w_out_b, norm_ffn, w_up_b, w_down_b, norm_final)
    param_specs = [_const_spec(p.shape) for p in params]
    d_in = w_in_b.shape[1]

    meta_hist, meta_state = pl.pallas_call(
        _meta_kernel,
        out_shape=(jax.ShapeDtypeStruct((CONV_WIDTH - 1, D_CONV), F32),
                   jax.ShapeDtypeStruct((RET_HEADS, RET_HEAD_DIM, RET_HEAD_DIM), F32)),
        compiler_params=pltpu.CompilerParams(vmem_limit_bytes=VMEM_LIMIT_BYTES),
        name="meta_state",
    )(meta_tokens, cos_m, sin_m, norm_mix, w_in_b)

    n_tiles = seq // PROMPT_TILE
    y_prompt, conv_p, ret_p = pl.pallas_call(
        _prompt_kernel,
        grid=(batch, n_tiles),
        in_specs=[
            pl.BlockSpec((1, PROMPT_TILE, D_MODEL), lambda b, i: (b, i, 0)),
            pl.BlockSpec((PROMPT_TILE, RET_HEAD_DIM), lambda b, i: (i, 0)),
            pl.BlockSpec((PROMPT_TILE, RET_HEAD_DIM), lambda b, i: (i, 0)),
            _const_spec(meta_hist.shape),
            _const_spec(meta_state.shape),
        ] + param_specs,
        out_specs=(
            pl.BlockSpec((1, PROMPT_TILE, D_MODEL), lambda b, i: (b, i, 0)),
            pl.BlockSpec((1, CONV_WIDTH - 1, D_CONV), lambda b, i: (b, 0, 0)),
            pl.BlockSpec((1, RET_HEADS, RET_HEAD_DIM, RET_HEAD_DIM), lambda b, i: (b, 0, 0, 0)),
        ),
        out_shape=(
            jax.ShapeDtypeStruct((batch, seq, D_MODEL), F32),
            jax.ShapeDtypeStruct((batch, CONV_WIDTH - 1, D_CONV), F32),
            jax.ShapeDtypeStruct((batch, RET_HEADS, RET_HEAD_DIM, RET_HEAD_DIM), F32),
        ),
        scratch_shapes=[
            pltpu.VMEM((1, PROMPT_TILE + SUBLANES, D_CONV), F32),
            pltpu.VMEM((RET_HEADS, RET_HEAD_DIM, RET_HEAD_DIM), F32),
        ],
        compiler_params=pltpu.CompilerParams(
            dimension_semantics=("arbitrary", "arbitrary"),
            vmem_limit_bytes=VMEM_LIMIT_BYTES),
        name="prompt_step",
    )(x_prompt, cos_p, sin_p, meta_hist, meta_state, *params)

    ns = SAMPLE_STREAMS
    y_sample, conv_s, ret_s = pl.pallas_call(
        _sample_kernel,
        grid=(dec_batch // ns,),
        in_specs=[
            pl.BlockSpec((ns, dec_seq, D_MODEL), lambda b: (b, 0, 0)),
            _const_spec(cos_s.shape),
            _const_spec(sin_s.shape),
            pl.BlockSpec((ns, CONV_WIDTH - 1, D_CONV), lambda b: (b, 0, 0)),
            pl.BlockSpec((ns, RET_HEADS, RET_HEAD_DIM, RET_HEAD_DIM), lambda b: (b, 0, 0, 0)),
        ] + param_specs,
        out_specs=(
            pl.BlockSpec((ns, dec_seq, D_MODEL), lambda b: (b, 0, 0)),
            pl.BlockSpec((ns, CONV_WIDTH - 1, D_CONV), lambda b: (b, 0, 0)),
            pl.BlockSpec((ns, RET_HEADS, RET_HEAD_DIM, RET_HEAD_DIM), lambda b: (b, 0, 0, 0)),
        ),
        out_shape=(
            jax.ShapeDtypeStruct((dec_batch, dec_seq, D_MODEL), F32),
            jax.ShapeDtypeStruct((dec_batch, CONV_WIDTH - 1, D_CONV), F32),
            jax.ShapeDtypeStruct((dec_batch, RET_HEADS, RET_HEAD_DIM, RET_HEAD_DIM), F32),
        ),
        scratch_shapes=[pltpu.VMEM((ns, dec_seq + SUBLANES, D_CONV), F32)],
        compiler_params=pltpu.CompilerParams(
            dimension_semantics=("arbitrary",),
            vmem_limit_bytes=VMEM_LIMIT_BYTES),
        name="sample_step",
    )(x_sample, cos_s, sin_s, state_conv[0], state_ret[0], *params)

    return (y_prompt, y_sample, conv_p[None], ret_p[None], conv_s[None], ret_s[None])
```

```python
import functools
import math

import jax
import jax.numpy as jnp
import numpy as np
from jax import lax
from jax.experimental import pallas as pl
from jax.experimental.pallas import tpu as pltpu

D_MODEL = 1024
N_META = 16
PAST_LEN = 2048
D_CONV = 512
CONV_WIDTH = 3
RET_HEADS = 4
RET_HEAD_DIM = 128
D_RET = RET_HEADS * RET_HEAD_DIM
D_FF = 4 * D_MODEL
ROPE_BASE = 10000.0
NORM_EPS = 1e-6
GN_EPS = 1e-5
K_SCALE = RET_HEAD_DIM ** -0.5
LOG_GAMMA = tuple(math.log(1.0 - 2.0 ** (-5.0 - h)) for h in range(RET_HEADS))

SUBLANES = 8
VMEM_LIMIT_BYTES = 61 * 1024 * 1024

PROMPT_TILE = 1024
PROMPT_CHUNK = 256
SAMPLE_STREAMS = 8
FF_BLOCK = 1024

BF16 = jnp.bfloat16
F32 = jnp.float32


def _dot(a, b):
    return jnp.dot(a, b, preferred_element_type=F32)


def _rmsnorm(x, w):
    ms = jnp.mean(x * x, axis=-1, keepdims=True)
    return x * lax.rsqrt(ms + NORM_EPS) * w


def _rotate(x, cos_t, sin_t):
    return x * cos_t + pltpu.roll(x, RET_HEAD_DIM // 2, 1) * sin_t


def _decay_tables(chunk):
    row = lax.broadcasted_iota(jnp.int32, (chunk, chunk), 0)
    col = lax.broadcasted_iota(jnp.int32, (chunk, chunk), 1)
    diff = (row - col).astype(F32)
    rowv = lax.broadcasted_iota(jnp.int32, (chunk, RET_HEAD_DIM), 0).astype(F32)
    tabs = []
    for lg in LOG_GAMMA:
        inner = jnp.where(diff >= 0.0, jnp.exp(jnp.maximum(diff, 0.0) * lg), 0.0) * K_SCALE
        cross = jnp.exp((rowv + 1.0) * lg)
        kdec = jnp.exp((chunk - 1.0 - rowv) * lg) * K_SCALE
        tabs.append((inner, cross, kdec, math.exp(chunk * lg)))
    return tabs


def _mixer_inputs(x, norm_w, w_in_ref, cos_t, sin_t):
    hn = _rmsnorm(x, norm_w).astype(BF16)
    pc = _dot(hn, w_in_ref[:, : 3 * D_CONV])
    xt = pc[:, :D_CONV]
    b_gate = pc[:, D_CONV: 2 * D_CONV]
    c_gate = pc[:, 2 * D_CONV:]
    u = c_gate * xt
    pr = _dot(hn, w_in_ref[:, 3 * D_CONV:])
    q, k, v, g = (pr[:, i * D_RET:(i + 1) * D_RET] for i in range(4))
    qh, kh, vh = [], [], []
    for h in range(RET_HEADS):
        sl = slice(h * RET_HEAD_DIM, (h + 1) * RET_HEAD_DIM)
        qh.append(_rotate(q[:, sl], cos_t, sin_t).astype(BF16))
        kh.append(_rotate(k[:, sl], cos_t, sin_t))
        vh.append(v[:, sl].astype(BF16))
    return u, b_gate, qh, kh, vh, g


def _retention_chunk(qc, kc, vc, s, tabs):
    inner_decay, cross_decay, k_decay, gamma_l = tabs
    scores = lax.dot_general(qc, kc.astype(BF16), (((1,), (1,)), ((), ())),
                             preferred_element_type=F32)
    p = (scores * inner_decay).astype(BF16)
    o = _dot(p, vc) + _dot(qc, s.astype(BF16)) * cross_decay
    kd = (kc * k_decay).astype(BF16)
    s_new = gamma_l * s + lax.dot_general(kd, vc, (((0,), (0,)), ((), ())),
                                          preferred_element_type=F32)
    return o, s_new


def _retention(qh, kh, vh, states, n_streams, seg, chunk, tabs):
    new_states = []
    rows = []
    for n in range(n_streams):
        st = list(states[n])
        chunk_rows = []
        for c in range(seg // chunk):
            lo = n * seg + c * chunk
            heads = []
            for h in range(RET_HEADS):
                o, st[h] = _retention_chunk(qh[h][lo:lo + chunk], kh[h][lo:lo + chunk],
                                            vh[h][lo:lo + chunk], st[h], tabs[h])
                heads.append(o)
            chunk_rows.append(heads)
        rows.extend(chunk_rows)
        new_states.append(st)
    o_heads = [jnp.concatenate([r[h] for r in rows], axis=0) if len(rows) > 1 else rows[0][h]
               for h in range(RET_HEADS)]
    return o_heads, new_states


def _group_norm_gate(o_heads, g, gn_w):
    outs = []
    for h in range(RET_HEADS):
        o = o_heads[h]
        mu = jnp.mean(o, axis=-1, keepdims=True)
        d = o - mu
        var = jnp.mean(d * d, axis=-1, keepdims=True)
        sl = slice(h * RET_HEAD_DIM, (h + 1) * RET_HEAD_DIM)
        gh = g[:, sl]
        outs.append((gh * jax.nn.sigmoid(gh)) * (d * lax.rsqrt(var + GN_EPS) * gn_w[:, sl]))
    return jnp.concatenate(outs, axis=1).astype(BF16)


def _ffn_and_final(h1, norm_ffn_w, w_up_ref, w_down_ref, norm_final_w):
    n2 = _rmsnorm(h1, norm_ffn_w).astype(BF16)
    f = None
    for j in range(D_FF // FF_BLOCK):
        sl = slice(j * FF_BLOCK, (j + 1) * FF_BLOCK)
        up = jnp.maximum(_dot(n2, w_up_ref[:, sl]), 0.0)
        d = _dot((up * up).astype(BF16), w_down_ref[sl, :])
        f = d if f is None else f + d
    return _rmsnorm(h1 + f, norm_final_w)


def _conv_group(u, b_gate, conv_w, ubuf_ref, n_streams, seg):
    ubuf_ref[:, SUBLANES:SUBLANES + seg, :] = u.reshape(n_streams, seg, D_CONV)
    u1 = ubuf_ref[:, SUBLANES - 1:SUBLANES - 1 + seg, :].reshape(n_streams * seg, D_CONV)
    u2 = ubuf_ref[:, SUBLANES - 2:SUBLANES - 2 + seg, :].reshape(n_streams * seg, D_CONV)
    conv_y = conv_w[0:1, :] * u2 + conv_w[1:2, :] * u1 + conv_w[2:3, :] * u
    new_hist = ubuf_ref[:, SUBLANES + seg - 2:SUBLANES + seg, :]
    return (b_gate * conv_y).astype(BF16), new_hist


def _layer(x, hist_ref_init, states, n_streams, seg, chunk, ubuf_ref, cos_t, sin_t,
           norm_mix_w, w_in_ref, conv_w, gn_w, w_out_ref, norm_ffn_w, w_up_ref, w_down_ref,
           norm_final_w):
    u, b_gate, qh, kh, vh, g = _mixer_inputs(x, norm_mix_w, w_in_ref, cos_t, sin_t)
    if hist_ref_init is not None:
        ubuf_ref[:, SUBLANES - 2:SUBLANES, :] = hist_ref_init
    conv_out, new_hist = _conv_group(u, b_gate, conv_w, ubuf_ref, n_streams, seg)
    tabs = _decay_tables(chunk)
    o_heads, new_states = _retention(qh, kh, vh, states, n_streams, seg, chunk, tabs)
    ret_out = _group_norm_gate(o_heads, g, gn_w)
    h1 = x + (_dot(conv_out, w_out_ref[:D_CONV, :]) + _dot(ret_out, w_out_ref[D_CONV:, :]))
    y = _ffn_and_final(h1, norm_ffn_w, w_up_ref, w_down_ref, norm_final_w)
    return y, new_hist, new_states


def _meta_kernel(meta_ref, cos_ref, sin_ref, norm_mix_ref, w_in_ref, hist_out_ref, state_out_ref):
    u, _, qh, kh, vh, _ = _mixer_inputs(meta_ref[...], norm_mix_ref[...], w_in_ref,
                                        cos_ref[...], sin_ref[...])
    hist_out_ref[...] = u[N_META - 2:N_META, :]
    tabs = _decay_tables(N_META)
    for h in range(RET_HEADS):
        _, _, k_decay, _ = tabs[h]
        kd = (kh[h] * k_decay).astype(BF16)
        state_out_ref[h] = lax.dot_general(kd, vh[h], (((0,), (0,)), ((), ())),
                                           preferred_element_type=F32)


def _prompt_kernel(x_ref, cos_ref, sin_ref, hist0_ref, state0_ref,
                   norm_mix_ref, w_in_ref, conv_w_ref, gn_w_ref, w_out_ref, norm_ffn_ref,
                   w_up_ref, w_down_ref, norm_final_ref,
                   y_ref, hist_out_ref, state_out_ref, ubuf_ref, state_ref):
    @pl.when(pl.program_id(1) == 0)
    def _():
        ubuf_ref[0, SUBLANES - 2:SUBLANES, :] = hist0_ref[...]
        state_ref[...] = state0_ref[...]

    states = [[state_ref[h] for h in range(RET_HEADS)]]
    y, new_hist, new_states = _layer(
        x_ref[0], None, states, 1, PROMPT_TILE, PROMPT_CHUNK, ubuf_ref,
        cos_ref[...], sin_ref[...], norm_mix_ref[...], w_in_ref, conv_w_ref[...],
        gn_w_ref[...], w_out_ref, norm_ffn_ref[...], w_up_ref, w_down_ref, norm_final_ref[...])
    y_ref[0] = y
    ubuf_ref[:, SUBLANES - 2:SUBLANES, :] = new_hist
    hist_out_ref[...] = new_hist
    for h in range(RET_HEADS):
        state_ref[h] = new_states[0][h]
        state_out_ref[0, h] = new_states[0][h]


def _sample_kernel(x_ref, cos_ref, sin_ref, hist_ref, state_in_ref,
                   norm_mix_ref, w_in_ref, conv_w_ref, gn_w_ref, w_out_ref, norm_ffn_ref,
                   w_up_ref, w_down_ref, norm_final_ref,
                   y_ref, hist_out_ref, state_out_ref, ubuf_ref):
    n, seg = SAMPLE_STREAMS, x_ref.shape[1]
    states = [[state_in_ref[i, h] for h in range(RET_HEADS)] for i in range(n)]
    cos_t = jnp.concatenate([cos_ref[...]] * n, axis=0)
    sin_t = jnp.concatenate([sin_ref[...]] * n, axis=0)
    y, new_hist, new_states = _layer(
        x_ref[...].reshape(n * seg, D_MODEL), hist_ref[...], states, n, seg, seg, ubuf_ref,
        cos_t, sin_t, norm_mix_ref[...], w_in_ref, conv_w_ref[...],
        gn_w_ref[...], w_out_ref, norm_ffn_ref[...], w_up_ref, w_down_ref, norm_final_ref[...])
    y_ref[...] = y.reshape(n, seg, D_MODEL)
    hist_out_ref[...] = new_hist
    for i in range(n):
        for h in range(RET_HEADS):
            state_out_ref[i, h] = new_states[i][h]


def _rope_tables(first_pos, n):
    half = RET_HEAD_DIM // 2
    inv_freq = ROPE_BASE ** (-np.arange(half, dtype=np.float64) / half)
    ang = (first_pos + np.arange(n, dtype=np.float64))[:, None] * inv_freq[None, :]
    cos, sin = np.cos(ang), np.sin(ang)
    return (jnp.asarray(np.concatenate([cos, cos], axis=-1), dtype=F32),
            jnp.asarray(np.concatenate([-sin, sin], axis=-1), dtype=F32))


def _const_spec(shape):
    nd = len(shape)
    return pl.BlockSpec(shape, lambda *_: (0,) * nd, pipeline_mode=pl.Buffered(1))


def kernel(x_prompt, x_sample, state_conv, state_ret, meta_tokens, norm_mix_w, w_in, conv_w,
           ret_gn_w, w_out, norm_ffn_w, w_up, w_down, norm_final_w):
    depth = w_in.shape[0]
    assert depth == 1, "single-layer step"
    batch, seq, _ = x_prompt.shape
    dec_batch, dec_seq, _ = x_sample.shape
    assert seq % PROMPT_TILE == 0 and dec_batch % SAMPLE_STREAMS == 0

    w_in_b, w_out_b = w_in[0].astype(BF16), w_out[0].astype(BF16)
    w_up_b, w_down_b = w_up[0].astype(BF16), w_down[0].astype(BF16)
    norm_mix, norm_ffn = norm_mix_w[0][None], norm_ffn_w[0][None]
    norm_final = norm_final_w[None]
    gn_w, cw = ret_gn_w[0][None], conv_w[0]

    cos_m, sin_m = _rope_tables(-N_META, N_META)
    cos_p, sin_p = _rope_tables(0, seq)
    cos_s, sin_s = _rope_tables(PAST_LEN, dec_seq)

    params = (norm_mix, w_in_b, cw, gn_w, w_out_b, norm_ffn, w_up_b, w_down_b, norm_final)
    param_specs = [_const_spec(p.shape) for p in params]
    d_in = w_in_b.shape[1]

    meta_hist, meta_state = pl.pallas_call(
        _meta_kernel,
        out_shape=(jax.ShapeDtypeStruct((CONV_WIDTH - 1, D_CONV), F32),
                   jax.ShapeDtypeStruct((RET_HEADS, RET_HEAD_DIM, RET_HEAD_DIM), F32)),
        compiler_params=pltpu.CompilerParams(vmem_limit_bytes=VMEM_LIMIT_BYTES),
        name="meta_state",
    )(meta_tokens, cos_m, sin_m, norm_mix, w_in_b)

    n_tiles = seq // PROMPT_TILE
    y_prompt, conv_p, ret_p = pl.pallas_call(
        _prompt_kernel,
        grid=(batch, n_tiles),
        in_specs=[
            pl.BlockSpec((1, PROMPT_TILE, D_MODEL), lambda b, i: (b, i, 0)),
            pl.BlockSpec((PROMPT_TILE, RET_HEAD_DIM), lambda b, i: (i, 0)),
            pl.BlockSpec((PROMPT_TILE, RET_HEAD_DIM), lambda b, i: (i, 0)),
            _const_spec(meta_hist.shape),
            _const_spec(meta_state.shape),
        ] + param_specs,
        out_specs=(
            pl.BlockSpec((1, PROMPT_TILE, D_MODEL), lambda b, i: (b, i, 0)),
            pl.BlockSpec((1, CONV_WIDTH - 1, D_CONV), lambda b, i: (b, 0, 0)),
            pl.BlockSpec((1, RET_HEADS, RET_HEAD_DIM, RET_HEAD_DIM), lambda b, i: (b, 0, 0, 0)),
        ),
        out_shape=(
            jax.ShapeDtypeStruct((batch, seq, D_MODEL), F32),
            jax.ShapeDtypeStruct((batch, CONV_WIDTH - 1, D_CONV), F32),
            jax.ShapeDtypeStruct((batch, RET_HEADS, RET_HEAD_DIM, RET_HEAD_DIM), F32),
        ),
        scratch_shapes=[
            pltpu.VMEM((1, PROMPT_TILE + SUBLANES, D_CONV), F32),
            pltpu.VMEM((RET_HEADS, RET_HEAD_DIM, RET_HEAD_DIM), F32),
        ],
        compiler_params=pltpu.CompilerParams(
            dimension_semantics=("arbitrary", "arbitrary"),
            vmem_limit_bytes=VMEM_LIMIT_BYTES),
        name="prompt_step",
    )(x_prompt, cos_p, sin_p, meta_hist, meta_state, *params)

    ns = SAMPLE_STREAMS
    y_sample, conv_s, ret_s = pl.pallas_call(
        _sample_kernel,
        grid=(dec_batch // ns,),
        in_specs=[
            pl.BlockSpec((ns, dec_seq, D_MODEL), lambda b: (b, 0, 0)),
            _const_spec(cos_s.shape),
            _const_spec(sin_s.shape),
            pl.BlockSpec((ns, CONV_WIDTH - 1, D_CONV), lambda b: (b, 0, 0)),
            pl.BlockSpec((ns, RET_HEADS, RET_HEAD_DIM, RET_HEAD_DIM), lambda b: (b, 0, 0, 0)),
        ] + param_specs,
        out_specs=(
            pl.BlockSpec((ns, dec_seq, D_MODEL), lambda b: (b, 0, 0)),
            pl.BlockSpec((ns, CONV_WIDTH - 1, D_CONV), lambda b: (b, 0, 0)),
            pl.BlockSpec((ns, RET_HEADS, RET_HEAD_DIM, RET_HEAD_DIM), lambda b: (b, 0, 0, 0)),
        ),
        out_shape=(
            jax.ShapeDtypeStruct((dec_batch, dec_seq, D_MODEL), F32),
            jax.ShapeDtypeStruct((dec_batch, CONV_WIDTH - 1, D_CONV), F32),
            jax.ShapeDtypeStruct((dec_batch, RET_HEADS, RET_HEAD_DIM, RET_HEAD_DIM), F32),
        ),
        scratch_shapes=[pltpu.VMEM((ns, dec_seq + SUBLANES, D_CONV), F32)],
        compiler_params=pltpu.CompilerParams(
            dimension_semantics=("arbitrary",),
            vmem_limit_bytes=VMEM_LIMIT_BYTES),
        name="sample_step",
    )(x_sample, cos_s, sin_s, state_conv[0], state_ret[0], *params)

    return (y_prompt, y_sample, conv_p[None], ret_p[None], conv_s[None], ret_s[None])
```

```python
import functools
import math

import jax
import jax.numpy as jnp
import numpy as np
from jax import lax
from jax.experimental import pallas as pl
from jax.experimental.pallas import tpu as pltpu

D_MODEL = 1024
N_META = 16
PAST_LEN = 2048
D_CONV = 512
CONV_WIDTH = 3
RET_HEADS = 4
RET_HEAD_DIM = 128
D_RET = RET_HEADS * RET_HEAD_DIM
D_FF = 4 * D_MODEL
ROPE_BASE = 10000.0
NORM_EPS = 1e-6
GN_EPS = 1e-5
K_SCALE = RET_HEAD_DIM ** -0.5
LOG_GAMMA = tuple(math.log(1.0 - 2.0 ** (-5.0 - h)) for h in range(RET_HEADS))

SUBLANES = 8
VMEM_LIMIT_BYTES = 58 * 1024 * 1024

PROMPT_TILE = 512
PROMPT_CHUNK = 256
SAMPLE_STREAMS = 8
FF_BLOCK = 1024

BF16 = jnp.bfloat16
F32 = jnp.float32


def _dot(a, b):
    return jnp.dot(a, b, preferred_element_type=F32)


def _rmsnorm(x, w):
    ms = jnp.mean(x * x, axis=-1, keepdims=True)
    return x * lax.rsqrt(ms + NORM_EPS) * w


def _rotate(x, cos_t, sin_t):
    return x * cos_t + pltpu.roll(x, RET_HEAD_DIM // 2, 1) * sin_t


def _decay_tables(chunk):
    row = lax.broadcasted_iota(jnp.int32, (chunk, chunk), 0)
    col = lax.broadcasted_iota(jnp.int32, (chunk, chunk), 1)
    diff = (row - col).astype(F32)
    rowv = lax.broadcasted_iota(jnp.int32, (chunk, RET_HEAD_DIM), 0).astype(F32)
    tabs = []
    for lg in LOG_GAMMA:
        inner = jnp.where(diff >= 0.0, jnp.exp(jnp.maximum(diff, 0.0) * lg), 0.0) * K_SCALE
        cross = jnp.exp((rowv + 1.0) * lg)
        kdec = jnp.exp((chunk - 1.0 - rowv) * lg) * K_SCALE
        tabs.append((inner, cross, kdec, math.exp(chunk * lg)))
    return tabs


def _mixer_inputs(x, norm_w, w_in_ref, cos_t, sin_t):
    hn = _rmsnorm(x, norm_w).astype(BF16)
    pc = _dot(hn, w_in_ref[:, : 3 * D_CONV])
    xt = pc[:, :D_CONV]
    b_gate = pc[:, D_CONV: 2 * D_CONV]
    c_gate = pc[:, 2 * D_CONV:]
    u = c_gate * xt
    pr = _dot(hn, w_in_ref[:, 3 * D_CONV:])
    q, k, v, g = (pr[:, i * D_RET:(i + 1) * D_RET] for i in range(4))
    qh, kh, vh = [], [], []
    for h in range(RET_HEADS):
        sl = slice(h * RET_HEAD_DIM, (h + 1) * RET_HEAD_DIM)
        qh.append(_rotate(q[:, sl], cos_t, sin_t).astype(BF16))
        kh.append(_rotate(k[:, sl], cos_t, sin_t))
        vh.append(v[:, sl].astype(BF16))
    return u, b_gate, qh, kh, vh, g


def _retention_chunk(qc, kc, vc, s, tabs):
    inner_decay, cross_decay, k_decay, gamma_l = tabs
    scores = lax.dot_general(qc, kc.astype(BF16), (((1,), (1,)), ((), ())),
                             preferred_element_type=F32)
    p = (scores * inner_decay).astype(BF16)
    o = _dot(p, vc) + _dot(qc, s.astype(BF16)) * cross_decay
    kd = (kc * k_decay).astype(BF16)
    s_new = gamma_l * s + lax.dot_general(kd, vc, (((0,), (0,)), ((), ())),
                                          preferred_element_type=F32)
    return o, s_new


def _retention(qh, kh, vh, states, n_streams, seg, chunk, tabs):
    new_states = []
    rows = []
    for n in range(n_streams):
        st = list(states[n])
        chunk_rows = []
        for c in range(seg // chunk):
            lo = n * seg + c * chunk
            heads = []
            for h in range(RET_HEADS):
                o, st[h] = _retention_chunk(qh[h][lo:lo + chunk], kh[h][lo:lo + chunk],
                                            vh[h][lo:lo + chunk], st[h], tabs[h])
                heads.append(o)
            chunk_rows.append(heads)
        rows.extend(chunk_rows)
        new_states.append(st)
    o_heads = [jnp.concatenate([r[h] for r in rows], axis=0) if len(rows) > 1 else rows[0][h]
               for h in range(RET_HEADS)]
    return o_heads, new_states


def _group_norm_gate(o_heads, g, gn_w):
    outs = []
    for h in range(RET_HEADS):
        o = o_heads[h]
        mu = jnp.mean(o, axis=-1, keepdims=True)
        d = o - mu
        var = jnp.mean(d * d, axis=-1, keepdims=True)
        sl = slice(h * RET_HEAD_DIM, (h + 1) * RET_HEAD_DIM)
        gh = g[:, sl]
        outs.append((gh * jax.nn.sigmoid(gh)) * (d * lax.rsqrt(var + GN_EPS) * gn_w[:, sl]))
    return jnp.concatenate(outs, axis=1).astype(BF16)


def _ffn_up(n2, w_up_ref, j):
    up = jnp.maximum(_dot(n2, w_up_ref[:, j * FF_BLOCK:(j + 1) * FF_BLOCK]), 0.0)
    return (up * up).astype(BF16)


def _ffn_down(hid, w_down_ref, j):
    return _dot(hid, w_down_ref[j * FF_BLOCK:(j + 1) * FF_BLOCK, :])


def _ffn_and_final(h1, norm_ffn_w, w_up_ref, w_down_ref, norm_final_w):
    n2 = _rmsnorm(h1, norm_ffn_w).astype(BF16)
    f = None
    for j in range(D_FF // FF_BLOCK):
        d = _ffn_down(_ffn_up(n2, w_up_ref, j), w_down_ref, j)
        f = d if f is None else f + d
    return _rmsnorm(h1 + f, norm_final_w)


def _conv_group(u, b_gate, conv_w, ubuf_ref, n_streams, seg):
    ubuf_ref[:, SUBLANES:SUBLANES + seg, :] = u.reshape(n_streams, seg, D_CONV)
    u1 = ubuf_ref[:, SUBLANES - 1:SUBLANES - 1 + seg, :].reshape(n_streams * seg, D_CONV)
    u2 = ubuf_ref[:, SUBLANES - 2:SUBLANES - 2 + seg, :].reshape(n_streams * seg, D_CONV)
    conv_y = conv_w[0:1, :] * u2 + conv_w[1:2, :] * u1 + conv_w[2:3, :] * u
    new_hist = ubuf_ref[:, SUBLANES + seg - 2:SUBLANES + seg, :]
    return (b_gate * conv_y).astype(BF16), new_hist


def _layer(x, hist_ref_init, states, n_streams, seg, chunk, ubuf_ref, cos_t, sin_t,
           norm_mix_w, w_in_ref, conv_w, gn_w, w_out_ref, norm_ffn_w, w_up_ref, w_down_ref,
           norm_final_w):
    u, b_gate, qh, kh, vh, g = _mixer_inputs(x, norm_mix_w, w_in_ref, cos_t, sin_t)
    if hist_ref_init is not None:
        ubuf_ref[:, SUBLANES - 2:SUBLANES, :] = hist_ref_init
    conv_out, new_hist = _conv_group(u, b_gate, conv_w, ubuf_ref, n_streams, seg)
    tabs = _decay_tables(chunk)
    o_heads, new_states = _retention(qh, kh, vh, states, n_streams, seg, chunk, tabs)
    ret_out = _group_norm_gate(o_heads, g, gn_w)
    h1 = x + (_dot(conv_out, w_out_ref[:D_CONV, :]) + _dot(ret_out, w_out_ref[D_CONV:, :]))
    y = _ffn_and_final(h1, norm_ffn_w, w_up_ref, w_down_ref, norm_final_w)
    return y, new_hist, new_states


def _meta_kernel(meta_ref, cos_ref, sin_ref, norm_mix_ref, w_in_ref, hist_out_ref, state_out_ref):
    u, _, qh, kh, vh, _ = _mixer_inputs(meta_ref[...], norm_mix_ref[...], w_in_ref,
                                        cos_ref[...], sin_ref[...])
    hist_out_ref[...] = u[N_META - 2:N_META, :]
    tabs = _decay_tables(N_META)
    for h in range(RET_HEADS):
        _, _, k_decay, _ = tabs[h]
        kd = (kh[h] * k_decay).astype(BF16)
        state_out_ref[h] = lax.dot_general(kd, vh[h], (((0,), (0,)), ((), ())),
                                           preferred_element_type=F32)


def _prompt_kernel(x_ref, cos_ref, sin_ref, hist0_ref, state0_ref,
                   norm_mix_ref, w_in_ref, conv_w_ref, gn_w_ref, w_out_ref, norm_ffn_ref,
                   w_up_ref, w_down_ref, norm_final_ref,
                   y_ref, hist_out_ref, state_out_ref, ubuf_ref, state_ref, h1_ref,
                   *, tiles_per_stream, n_tiles):
    t = pl.program_id(0)

    @pl.when(t == 0)
    def _():
        h1_ref[...] = jnp.zeros_like(h1_ref)

    @pl.when(t % tiles_per_stream == 0)
    def _():
        ubuf_ref[0, SUBLANES - 2:SUBLANES, :] = hist0_ref[...]
        state_ref[...] = state0_ref[...]

    x = x_ref[0]
    h1_prev = h1_ref[...]
    n2 = _rmsnorm(h1_prev, norm_ffn_ref[...]).astype(BF16)
    u, b_gate, qh, kh, vh, g = _mixer_inputs(x, norm_mix_ref[...], w_in_ref,
                                             cos_ref[...], sin_ref[...])
    conv_out, new_hist = _conv_group(u, b_gate, conv_w_ref[...], ubuf_ref, 1, PROMPT_TILE)
    tabs = _decay_tables(PROMPT_CHUNK)

    n_chunks = PROMPT_TILE // PROMPT_CHUNK
    assert n_chunks * RET_HEADS == 2 * (D_FF // FF_BLOCK)
    st = [state_ref[h] for h in range(RET_HEADS)]
    o_chunks = [[None] * n_chunks for _ in range(RET_HEADS)]
    f, hid, piece = None, None, 0
    for c in range(n_chunks):
        rows = slice(c * PROMPT_CHUNK, (c + 1) * PROMPT_CHUNK)
        for h in range(RET_HEADS):
            o_chunks[h][c], st[h] = _retention_chunk(qh[h][rows], kh[h][rows], vh[h][rows],
                                                     st[h], tabs[h])
            if piece % 2 == 0:
                hid = _ffn_up(n2, w_up_ref, piece // 2)
            else:
                d = _ffn_down(hid, w_down_ref, piece // 2)
                f = d if f is None else f + d
            piece += 1
    y_ref[0] = _rmsnorm(h1_prev + f, norm_final_ref[...])

    o_heads = [jnp.concatenate(o_chunks[h], axis=0) for h in range(RET_HEADS)]
    ret_out = _group_norm_gate(o_heads, g, gn_w_ref[...])
    h1_ref[...] = x + (_dot(conv_out, w_out_ref[:D_CONV, :]) + _dot(ret_out, w_out_ref[D_CONV:, :]))
    ubuf_ref[:, SUBLANES - 2:SUBLANES, :] = new_hist
    for h in range(RET_HEADS):
        state_ref[h] = st[h]

    @pl.when(t < n_tiles)
    def _():
        hist_out_ref[...] = new_hist
        for h in range(RET_HEADS):
            state_out_ref[0, h] = st[h]


def _sample_kernel(x_ref, cos_ref, sin_ref, hist_ref, state_in_ref,
                   norm_mix_ref, w_in_ref, conv_w_ref, gn_w_ref, w_out_ref, norm_ffn_ref,
                   w_up_ref, w_down_ref, norm_final_ref,
                   y_ref, hist_out_ref, state_out_ref, ubuf_ref):
    n, seg = SAMPLE_STREAMS, x_ref.shape[1]
    states = [[state_in_ref[i, h] for h in range(RET_HEADS)] for i in range(n)]
    cos_t = jnp.concatenate([cos_ref[...]] * n, axis=0)
    sin_t = jnp.concatenate([sin_ref[...]] * n, axis=0)
    y, new_hist, new_states = _layer(
        x_ref[...].reshape(n * seg, D_MODEL), hist_ref[...], states, n, seg, seg, ubuf_ref,
        cos_t, sin_t, norm_mix_ref[...], w_in_ref, conv_w_ref[...],
        gn_w_ref[...], w_out_ref, norm_ffn_ref[...], w_up_ref, w_down_ref, norm_final_ref[...])
    y_ref[...] = y.reshape(n, seg, D_MODEL)
    hist_out_ref[...] = new_hist
    for i in range(n):
        for h in range(RET_HEADS):
            state_out_ref[i, h] = new_states[i][h]


def _rope_tables(first_pos, n):
    half = RET_HEAD_DIM // 2
    inv_freq = ROPE_BASE ** (-np.arange(half, dtype=np.float64) / half)
    ang = (first_pos + np.arange(n, dtype=np.float64))[:, None] * inv_freq[None, :]
    cos, sin = np.cos(ang), np.sin(ang)
    return (jnp.asarray(np.concatenate([cos, cos], axis=-1), dtype=F32),
            jnp.asarray(np.concatenate([-sin, sin], axis=-1), dtype=F32))


def _const_spec(shape):
    nd = len(shape)
    return pl.BlockSpec(shape, lambda *_: (0,) * nd, pipeline_mode=pl.Buffered(1))


def kernel(x_prompt, x_sample, state_conv, state_ret, meta_tokens, norm_mix_w, w_in, conv_w,
           ret_gn_w, w_out, norm_ffn_w, w_up, w_down, norm_final_w):
    depth = w_in.shape[0]
    assert depth == 1, "single-layer step"
    batch, seq, _ = x_prompt.shape
    dec_batch, dec_seq, _ = x_sample.shape
    assert seq % PROMPT_TILE == 0 and dec_batch % SAMPLE_STREAMS == 0

    w_in_b, w_out_b = w_in[0].astype(BF16), w_out[0].astype(BF16)
    w_up_b, w_down_b = w_up[0].astype(BF16), w_down[0].astype(BF16)
    norm_mix, norm_ffn = norm_mix_w[0][None], norm_ffn_w[0][None]
    norm_final = norm_final_w[None]
    gn_w, cw = ret_gn_w[0][None], conv_w[0]

    cos_m, sin_m = _rope_tables(-N_META, N_META)
    cos_p, sin_p = _rope_tables(0, seq)
    cos_s, sin_s = _rope_tables(PAST_LEN, dec_seq)

    params = (norm_mix, w_in_b, cw, gn_w, w_out_b, norm_ffn, w_up_b, w_down_b, norm_final)
    param_specs = [_const_spec(p.shape) for p in params]
    d_in = w_in_b.shape[1]

    meta_hist, meta_state = pl.pallas_call(
        _meta_kernel,
        out_shape=(jax.ShapeDtypeStruct((CONV_WIDTH - 1, D_CONV), F32),
                   jax.ShapeDtypeStruct((RET_HEADS, RET_HEAD_DIM, RET_HEAD_DIM), F32)),
        compiler_params=pltpu.CompilerParams(vmem_limit_bytes=VMEM_LIMIT_BYTES),
        name="meta_state",
    )(meta_tokens, cos_m, sin_m, norm_mix, w_in_b)

    tiles_per_stream = seq // PROMPT_TILE
    n_tiles = batch * tiles_per_stream

    def cur(t):
        return jnp.minimum(t, n_tiles - 1)

    def prev(t):
        return jnp.maximum(t - 1, 0)

    y_prompt, conv_p, ret_p = pl.pallas_call(
        functools.partial(_prompt_kernel, tiles_per_stream=tiles_per_stream, n_tiles=n_tiles),
        grid=(n_tiles + 1,),
        in_specs=[
            pl.BlockSpec((1, PROMPT_TILE, D_MODEL),
                         lambda t: (cur(t) // tiles_per_stream, cur(t) % tiles_per_stream, 0)),
            pl.BlockSpec((PROMPT_TILE, RET_HEAD_DIM), lambda t: (cur(t) % tiles_per_stream, 0)),
            pl.BlockSpec((PROMPT_TILE, RET_HEAD_DIM), lambda t: (cur(t) % tiles_per_stream, 0)),
            _const_spec(meta_hist.shape),
            _const_spec(meta_state.shape),
        ] + param_specs,
        out_specs=(
            pl.BlockSpec((1, PROMPT_TILE, D_MODEL),
                         lambda t: (prev(t) // tiles_per_stream, prev(t) % tiles_per_stream, 0)),
            pl.BlockSpec((1, CONV_WIDTH - 1, D_CONV), lambda t: (cur(t) // tiles_per_stream, 0, 0)),
            pl.BlockSpec((1, RET_HEADS, RET_HEAD_DIM, RET_HEAD_DIM),
                         lambda t: (cur(t) // tiles_per_stream, 0, 0, 0)),
        ),
        out_shape=(
            jax.ShapeDtypeStruct((batch, seq, D_MODEL), F32),
            jax.ShapeDtypeStruct((batch, CONV_WIDTH - 1, D_CONV), F32),
            jax.ShapeDtypeStruct((batch, RET_HEADS, RET_HEAD_DIM, RET_HEAD_DIM), F32),
        ),
        scratch_shapes=[
            pltpu.VMEM((1, PROMPT_TILE + SUBLANES, D_CONV), F32),
            pltpu.VMEM((RET_HEADS, RET_HEAD_DIM, RET_HEAD_DIM), F32),
            pltpu.VMEM((PROMPT_TILE, D_MODEL), F32),
        ],
        compiler_params=pltpu.CompilerParams(
            dimension_semantics=("arbitrary",),
            vmem_limit_bytes=VMEM_LIMIT_BYTES),
        name="prompt_step",
    )(x_prompt, cos_p, sin_p, meta_hist, meta_state, *params)

    ns = SAMPLE_STREAMS
    y_sample, conv_s, ret_s = pl.pallas_call(
        _sample_kernel,
        grid=(dec_batch // ns,),
        in_specs=[
            pl.BlockSpec((ns, dec_seq, D_MODEL), lambda b: (b, 0, 0)),
            _const_spec(cos_s.shape),
            _const_spec(sin_s.shape),
            pl.BlockSpec((ns, CONV_WIDTH - 1, D_CONV), lambda b: (b, 0, 0)),
            pl.BlockSpec((ns, RET_HEADS, RET_HEAD_DIM, RET_HEAD_DIM), lambda b: (b, 0, 0, 0)),
        ] + param_specs,
        out_specs=(
            pl.BlockSpec((ns, dec_seq, D_MODEL), lambda b: (b, 0, 0)),
            pl.BlockSpec((ns, CONV_WIDTH - 1, D_CONV), lambda b: (b, 0, 0)),
            pl.BlockSpec((ns, RET_HEADS, RET_HEAD_DIM, RET_HEAD_DIM), lambda b: (b, 0, 0, 0)),
        ),
        out_shape=(
            jax.ShapeDtypeStruct((dec_batch, dec_seq, D_MODEL), F32),
            jax.ShapeDtypeStruct((dec_batch, CONV_WIDTH - 1, D_CONV), F32),
            jax.ShapeDtypeStruct((dec_batch, RET_HEADS, RET_HEAD_DIM, RET_HEAD_DIM), F32),
        ),
        scratch_shapes=[pltpu.VMEM((ns, dec_seq + SUBLANES, D_CONV), F32)],
        compiler_params=pltpu.CompilerParams(
            dimension_semantics=("arbitrary",),
            vmem_limit_bytes=VMEM_LIMIT_BYTES),
        name="sample_step",
    )(x_sample, cos_s, sin_s, state_conv[0], state_ret[0], *params)

    return (y_prompt, y_sample, conv_p[None], ret_p[None], conv_s[None], ret_s[None])
```

```python
import functools
import math

import jax
import jax.numpy as jnp
import numpy as np
from jax import lax
from jax.experimental import pallas as pl
from jax.experimental.pallas import tpu as pltpu

D_MODEL = 1024
N_META = 16
PAST_LEN = 2048
D_CONV = 512
CONV_WIDTH = 3
RET_HEADS = 4
RET_HEAD_DIM = 128
D_RET = RET_HEADS * RET_HEAD_DIM
D_FF = 4 * D_MODEL
ROPE_BASE = 10000.0
NORM_EPS = 1e-6
GN_EPS = 1e-5
K_SCALE = RET_HEAD_DIM ** -0.5
LOG_GAMMA = tuple(math.log(1.0 - 2.0 ** (-5.0 - h)) for h in range(RET_HEADS))

SUBLANES = 8
VMEM_LIMIT_BYTES = 62 * 1024 * 1024

PROMPT_TILE = 512
PROMPT_CHUNK = 256
SAMPLE_STREAMS = 8
FF_BLOCK = 1024
FF_UNIT = 256
FFN_TAIL_UNITS = 4
CONV_OUT_AFTER_UNITS = 4
NEXT_PROJ_TAIL_UNITS = 3

BF16 = jnp.bfloat16
F32 = jnp.float32


def _dot(a, b):
    return jnp.dot(a, b, preferred_element_type=F32)


def _rmsnorm(x, w):
    ms = jnp.mean(x * x, axis=-1, keepdims=True)
    return x * lax.rsqrt(ms + NORM_EPS) * w


def _rotate(x, cos_t, sin_t):
    return x * cos_t + pltpu.roll(x, RET_HEAD_DIM // 2, 1) * sin_t


def _decay_tables(chunk):
    row = lax.broadcasted_iota(jnp.int32, (chunk, chunk), 0)
    col = lax.broadcasted_iota(jnp.int32, (chunk, chunk), 1)
    diff = (row - col).astype(F32)
    rowv = lax.broadcasted_iota(jnp.int32, (chunk, RET_HEAD_DIM), 0).astype(F32)
    tabs = []
    for lg in LOG_GAMMA:
        inner = jnp.where(diff >= 0.0, jnp.exp(jnp.maximum(diff, 0.0) * lg), 0.0) * K_SCALE
        cross = jnp.exp((rowv + 1.0) * lg)
        kdec = jnp.exp((chunk - 1.0 - rowv) * lg) * K_SCALE
        tabs.append((inner, cross, kdec, math.exp(chunk * lg)))
    return tabs


def _conv_projection(hn, w_in_ref):
    return _dot(hn, w_in_ref[:, : 3 * D_CONV])


def _conv_inputs(pc):
    return pc[:, 2 * D_CONV:] * pc[:, :D_CONV], pc[:, D_CONV: 2 * D_CONV]


def _retention_inputs(hn, w_in_ref, cos_t, sin_t):
    pr = _dot(hn, w_in_ref[:, 3 * D_CONV:])
    q, k, v, g = (pr[:, i * D_RET:(i + 1) * D_RET] for i in range(4))
    qh, kh, vh = [], [], []
    for h in range(RET_HEADS):
        sl = slice(h * RET_HEAD_DIM, (h + 1) * RET_HEAD_DIM)
        qh.append(_rotate(q[:, sl], cos_t, sin_t).astype(BF16))
        kh.append(_rotate(k[:, sl], cos_t, sin_t))
        vh.append(v[:, sl].astype(BF16))
    return qh, kh, vh, g


def _mixer_inputs(hn, w_in_ref, cos_t, sin_t):
    u, b_gate = _conv_inputs(_conv_projection(hn, w_in_ref))
    return (u, b_gate) + _retention_inputs(hn, w_in_ref, cos_t, sin_t)


def _retention_scores(qc, kc, vc, s, tabs):
    _, _, k_decay, gamma_l = tabs
    scores = lax.dot_general(qc, kc.astype(BF16), (((1,), (1,)), ((), ())),
                             preferred_element_type=F32)
    kd = (kc * k_decay).astype(BF16)
    s_new = gamma_l * s + lax.dot_general(kd, vc, (((0,), (0,)), ((), ())),
                                          preferred_element_type=F32)
    return scores, s_new


def _retention_output(scores, qc, vc, s, tabs):
    inner_decay, cross_decay, _, _ = tabs
    p = (scores * inner_decay).astype(BF16)
    return _dot(p, vc) + _dot(qc, s.astype(BF16)) * cross_decay


def _retention_chunk(qc, kc, vc, s, tabs):
    scores, s_new = _retention_scores(qc, kc, vc, s, tabs)
    return _retention_output(scores, qc, vc, s, tabs), s_new


def _retention(qh, kh, vh, states, n_streams, seg, chunk, tabs):
    new_states = []
    rows = []
    for n in range(n_streams):
        st = list(states[n])
        chunk_rows = []
        for c in range(seg // chunk):
            lo = n * seg + c * chunk
            heads = []
            for h in range(RET_HEADS):
                o, st[h] = _retention_chunk(qh[h][lo:lo + chunk], kh[h][lo:lo + chunk],
                                            vh[h][lo:lo + chunk], st[h], tabs[h])
                heads.append(o)
            chunk_rows.append(heads)
        rows.extend(chunk_rows)
        new_states.append(st)
    o_heads = [jnp.concatenate([r[h] for r in rows], axis=0) if len(rows) > 1 else rows[0][h]
               for h in range(RET_HEADS)]
    return o_heads, new_states


def _group_norm_gate(o_heads, g, gn_w):
    outs = []
    for h in range(RET_HEADS):
        o = o_heads[h]
        mu = jnp.mean(o, axis=-1, keepdims=True)
        d = o - mu
        var = jnp.mean(d * d, axis=-1, keepdims=True)
        sl = slice(h * RET_HEAD_DIM, (h + 1) * RET_HEAD_DIM)
        gh = g[:, sl]
        outs.append((gh * jax.nn.sigmoid(gh)) * (d * lax.rsqrt(var + GN_EPS) * gn_w[:, sl]))
    return jnp.concatenate(outs, axis=1).astype(BF16)


def _ffn_up(n2, w_up_ref, j):
    up = jnp.maximum(_dot(n2, w_up_ref[:, j * FF_BLOCK:(j + 1) * FF_BLOCK]), 0.0)
    return (up * up).astype(BF16)


def _ffn_down(hid, w_down_ref, j):
    return _dot(hid, w_down_ref[j * FF_BLOCK:(j + 1) * FF_BLOCK, :])


def _ffn_and_final(h1, norm_ffn_w, w_up_ref, w_down_ref, norm_final_w):
    n2 = _rmsnorm(h1, norm_ffn_w).astype(BF16)
    f = None
    for j in range(D_FF // FF_BLOCK):
        d = _ffn_down(_ffn_up(n2, w_up_ref, j), w_down_ref, j)
        f = d if f is None else f + d
    return _rmsnorm(h1 + f, norm_final_w)


def _conv_group(u, b_gate, conv_w, ubuf_ref, n_streams, seg):
    ubuf_ref[:, SUBLANES:SUBLANES + seg, :] = u.reshape(n_streams, seg, D_CONV)
    u1 = ubuf_ref[:, SUBLANES - 1:SUBLANES - 1 + seg, :].reshape(n_streams * seg, D_CONV)
    u2 = ubuf_ref[:, SUBLANES - 2:SUBLANES - 2 + seg, :].reshape(n_streams * seg, D_CONV)
    conv_y = conv_w[0:1, :] * u2 + conv_w[1:2, :] * u1 + conv_w[2:3, :] * u
    new_hist = ubuf_ref[:, SUBLANES + seg - 2:SUBLANES + seg, :]
    return (b_gate * conv_y).astype(BF16), new_hist


def _layer(x, hist_ref_init, states, n_streams, seg, chunk, ubuf_ref, cos_t, sin_t,
           norm_mix_w, w_in_ref, conv_w, gn_w, w_out_ref, norm_ffn_w, w_up_ref, w_down_ref,
           norm_final_w):
    hn = _rmsnorm(x, norm_mix_w).astype(BF16)
    u, b_gate, qh, kh, vh, g = _mixer_inputs(hn, w_in_ref, cos_t, sin_t)
    if hist_ref_init is not None:
        ubuf_ref[:, SUBLANES - 2:SUBLANES, :] = hist_ref_init
    conv_out, new_hist = _conv_group(u, b_gate, conv_w, ubuf_ref, n_streams, seg)
    tabs = _decay_tables(chunk)
    o_heads, new_states = _retention(qh, kh, vh, states, n_streams, seg, chunk, tabs)
    ret_out = _group_norm_gate(o_heads, g, gn_w)
    h1 = x + (_dot(conv_out, w_out_ref[:D_CONV, :]) + _dot(ret_out, w_out_ref[D_CONV:, :]))
    y = _ffn_and_final(h1, norm_ffn_w, w_up_ref, w_down_ref, norm_final_w)
    return y, new_hist, new_states


def _meta_kernel(meta_ref, cos_ref, sin_ref, norm_mix_ref, w_in_ref, hist_out_ref, state_out_ref):
    hn = _rmsnorm(meta_ref[...], norm_mix_ref[...]).astype(BF16)
    u, _, qh, kh, vh, _ = _mixer_inputs(hn, w_in_ref, cos_ref[...], sin_ref[...])
    hist_out_ref[...] = u[N_META - 2:N_META, :]
    tabs = _decay_tables(N_META)
    for h in range(RET_HEADS):
        _, _, k_decay, _ = tabs[h]
        kd = (kh[h] * k_decay).astype(BF16)
        state_out_ref[h] = lax.dot_general(kd, vh[h], (((0,), (0,)), ((), ())),
                                           preferred_element_type=F32)


def _prompt_kernel(x_ref, x_next_ref, cos_ref, sin_ref, hist0_ref, state0_ref,
                   norm_mix_ref, w_in_ref, conv_w_ref, gn_w_ref, w_out_ref, norm_ffn_ref,
                   w_up_ref, w_down_ref, norm_final_ref,
                   y_ref, hist_out_ref, state_out_ref, ubuf_ref, state_ref, h1_ref, hn_ref, pc_ref,
                   *, tiles_per_stream, n_tiles):
    t = pl.program_id(0)

    wr = t % 2
    rd = 1 - wr

    @pl.when(t == 0)
    def _():
        h1_ref[...] = jnp.zeros_like(h1_ref)
        hn0 = _rmsnorm(x_ref[0], norm_mix_ref[...]).astype(BF16)
        hn_ref[...] = hn0
        pc_ref[...] = _conv_projection(hn0, w_in_ref)

    @pl.when(t % tiles_per_stream == 0)
    def _():
        ubuf_ref[0, SUBLANES - 2:SUBLANES, :] = hist0_ref[...]
        state_ref[...] = state0_ref[...]

    qh, kh, vh, g = _retention_inputs(hn_ref[...], w_in_ref, cos_ref[...], sin_ref[...])
    hn_next = _rmsnorm(x_next_ref[0], norm_mix_ref[...]).astype(BF16)
    u, b_gate = _conv_inputs(pc_ref[...])

    def next_conv_projection(q):
        cols = slice(q * FF_UNIT, (q + 1) * FF_UNIT)
        pc_ref[:, cols] = _dot(hn_next, w_in_ref[:, cols])

    n_next = 3 * D_CONV // FF_UNIT
    for q in range(n_next - NEXT_PROJ_TAIL_UNITS):
        next_conv_projection(q)
    hn_ref[...] = hn_next

    n2 = _rmsnorm(h1_ref[rd], norm_ffn_ref[...]).astype(BF16)
    conv_out, new_hist = _conv_group(u, b_gate, conv_w_ref[...], ubuf_ref, 1, PROMPT_TILE)
    tabs = _decay_tables(PROMPT_CHUNK)

    n_chunks = PROMPT_TILE // PROMPT_CHUNK
    n_blocks = D_FF // FF_BLOCK
    n_cols = FF_BLOCK // FF_UNIT
    hid_cols = [[None] * n_cols for _ in range(n_blocks)]
    f_cols = [None] * (D_MODEL // FF_UNIT)

    def up_unit(j, q):
        lo = j * FF_BLOCK + q * FF_UNIT
        up = jnp.maximum(_dot(n2, w_up_ref[:, lo:lo + FF_UNIT]), 0.0)
        hid_cols[j][q] = (up * up).astype(BF16)

    def down_unit(j, q):
        if q == 0:
            hid_cols[j] = [jnp.concatenate(hid_cols[j], axis=1)]
        d = _dot(hid_cols[j][0], w_down_ref[j * FF_BLOCK:(j + 1) * FF_BLOCK,
                                            q * FF_UNIT:(q + 1) * FF_UNIT])
        f_cols[q] = d if f_cols[q] is None else f_cols[q] + d

    blocks = [(up_unit, 0)]
    for j in range(1, n_blocks):
        blocks += [(up_unit, j), (down_unit, j - 1)]
    blocks.append((down_unit, n_blocks - 1))
    units = [functools.partial(fn, j, q) for fn, j in blocks
             for q in range(n_cols if fn is up_unit else D_MODEL // FF_UNIT)]
    units.reverse()

    def dense(n):
        for _ in range(n):
            units.pop()()

    n_pieces = n_chunks * RET_HEADS
    n_units = len(units)
    per_piece, after_chunk = 1, 2
    n_lead = n_units - n_pieces * per_piece - n_chunks * after_chunk - FFN_TAIL_UNITS
    dense(CONV_OUT_AFTER_UNITS)
    out_conv = _dot(conv_out, w_out_ref[:D_CONV, :])
    dense(n_lead - CONV_OUT_AFTER_UNITS)
    st = [state_ref[h] for h in range(RET_HEADS)]
    ret_out = [None] * n_chunks
    for c in range(n_chunks):
        rows = slice(c * PROMPT_CHUNK, (c + 1) * PROMPT_CHUNK)
        o_heads = []
        for h in range(RET_HEADS):
            scores, s_new = _retention_scores(qh[h][rows], kh[h][rows], vh[h][rows], st[h], tabs[h])
            dense(per_piece)
            o_heads.append(_retention_output(scores, qh[h][rows], vh[h][rows], st[h], tabs[h]))
            st[h] = s_new
        dense(after_chunk)
        ret_out[c] = _group_norm_gate(o_heads, g[rows], gn_w_ref[...])
    dense(FFN_TAIL_UNITS)
    assert not units
    h1_ref[wr] = x_ref[0] + (out_conv + _dot(jnp.concatenate(ret_out, axis=0), w_out_ref[D_CONV:, :]))
    y_ref[0] = _rmsnorm(h1_ref[rd] + jnp.concatenate(f_cols, axis=1), norm_final_ref[...])
    for q in range(n_next - NEXT_PROJ_TAIL_UNITS, n_next):
        next_conv_projection(q)
    ubuf_ref[:, SUBLANES - 2:SUBLANES, :] = new_hist
    for h in range(RET_HEADS):
        state_ref[h] = st[h]

    @pl.when(t < n_tiles)
    def _():
        hist_out_ref[...] = new_hist
        for h in range(RET_HEADS):
            state_out_ref[0, h] = st[h]


def _sample_kernel(x_ref, cos_ref, sin_ref, hist_ref, state_in_ref,
                   norm_mix_ref, w_in_ref, conv_w_ref, gn_w_ref, w_out_ref, norm_ffn_ref,
                   w_up_ref, w_down_ref, norm_final_ref,
                   y_ref, hist_out_ref, state_out_ref, ubuf_ref):
    n, seg = SAMPLE_STREAMS, x_ref.shape[1]
    states = [[state_in_ref[i, h] for h in range(RET_HEADS)] for i in range(n)]
    cos_t = jnp.concatenate([cos_ref[...]] * n, axis=0)
    sin_t = jnp.concatenate([sin_ref[...]] * n, axis=0)
    y, new_hist, new_states = _layer(
        x_ref[...].reshape(n * seg, D_MODEL), hist_ref[...], states, n, seg, seg, ubuf_ref,
        cos_t, sin_t, norm_mix_ref[...], w_in_ref, conv_w_ref[...],
        gn_w_ref[...], w_out_ref, norm_ffn_ref[...], w_up_ref, w_down_ref, norm_final_ref[...])
    y_ref[...] = y.reshape(n, seg, D_MODEL)
    hist_out_ref[...] = new_hist
    for i in range(n):
        for h in range(RET_HEADS):
            state_out_ref[i, h] = new_states[i][h]


def _rope_tables(first_pos, n):
    half = RET_HEAD_DIM // 2
    inv_freq = ROPE_BASE ** (-np.arange(half, dtype=np.float64) / half)
    ang = (first_pos + np.arange(n, dtype=np.float64))[:, None] * inv_freq[None, :]
    cos, sin = np.cos(ang), np.sin(ang)
    return (jnp.asarray(np.concatenate([cos, cos], axis=-1), dtype=F32),
            jnp.asarray(np.concatenate([-sin, sin], axis=-1), dtype=F32))


def _const_spec(shape):
    nd = len(shape)
    return pl.BlockSpec(shape, lambda *_: (0,) * nd, pipeline_mode=pl.Buffered(1))


def kernel(x_prompt, x_sample, state_conv, state_ret, meta_tokens, norm_mix_w, w_in, conv_w,
           ret_gn_w, w_out, norm_ffn_w, w_up, w_down, norm_final_w):
    depth = w_in.shape[0]
    assert depth == 1, "single-layer step"
    batch, seq, _ = x_prompt.shape
    dec_batch, dec_seq, _ = x_sample.shape
    assert seq % PROMPT_TILE == 0 and dec_batch % SAMPLE_STREAMS == 0

    w_in_b, w_out_b = w_in[0].astype(BF16), w_out[0].astype(BF16)
    w_up_b, w_down_b = w_up[0].astype(BF16), w_down[0].astype(BF16)
    norm_mix, norm_ffn = norm_mix_w[0][None], norm_ffn_w[0][None]
    norm_final = norm_final_w[None]
    gn_w, cw = ret_gn_w[0][None], conv_w[0]

    cos_m, sin_m = _rope_tables(-N_META, N_META)
    cos_p, sin_p = _rope_tables(0, seq)
    cos_s, sin_s = _rope_tables(PAST_LEN, dec_seq)

    params = (norm_mix, w_in_b, cw, gn_w, w_out_b, norm_ffn, w_up_b, w_down_b, norm_final)
    param_specs = [_const_spec(p.shape) for p in params]
    d_in = w_in_b.shape[1]

    meta_hist, meta_state = pl.pallas_call(
        _meta_kernel,
        out_shape=(jax.ShapeDtypeStruct((CONV_WIDTH - 1, D_CONV), F32),
                   jax.ShapeDtypeStruct((RET_HEADS, RET_HEAD_DIM, RET_HEAD_DIM), F32)),
        compiler_params=pltpu.CompilerParams(vmem_limit_bytes=VMEM_LIMIT_BYTES),
        name="meta_state",
    )(meta_tokens, cos_m, sin_m, norm_mix, w_in_b)

    tiles_per_stream = seq // PROMPT_TILE
    n_tiles = batch * tiles_per_stream

    def cur(t):
        return jnp.minimum(t, n_tiles - 1)

    def prev(t):
        return jnp.maximum(t - 1, 0)

    y_prompt, conv_p, ret_p = pl.pallas_call(
        functools.partial(_prompt_kernel, tiles_per_stream=tiles_per_stream, n_tiles=n_tiles),
        grid=(n_tiles + 1,),
        in_specs=[
            pl.BlockSpec((1, PROMPT_TILE, D_MODEL),
                         lambda t: (cur(t) // tiles_per_stream, cur(t) % tiles_per_stream, 0)),
            pl.BlockSpec((1, PROMPT_TILE, D_MODEL),
                         lambda t: (cur(t + 1) // tiles_per_stream, cur(t + 1) % tiles_per_stream, 0)),
            pl.BlockSpec((PROMPT_TILE, RET_HEAD_DIM), lambda t: (cur(t) % tiles_per_stream, 0)),
            pl.BlockSpec((PROMPT_TILE, RET_HEAD_DIM), lambda t: (cur(t) % tiles_per_stream, 0)),
            _const_spec(meta_hist.shape),
            _const_spec(meta_state.shape),
        ] + param_specs,
        out_specs=(
            pl.BlockSpec((1, PROMPT_TILE, D_MODEL),
                         lambda t: (prev(t) // tiles_per_stream, prev(t) % tiles_per_stream, 0)),
            pl.BlockSpec((1, CONV_WIDTH - 1, D_CONV), lambda t: (cur(t) // tiles_per_stream, 0, 0)),
            pl.BlockSpec((1, RET_HEADS, RET_HEAD_DIM, RET_HEAD_DIM),
                         lambda t: (cur(t) // tiles_per_stream, 0, 0, 0)),
        ),
        out_shape=(
            jax.ShapeDtypeStruct((batch, seq, D_MODEL), F32),
            jax.ShapeDtypeStruct((batch, CONV_WIDTH - 1, D_CONV), F32),
            jax.ShapeDtypeStruct((batch, RET_HEADS, RET_HEAD_DIM, RET_HEAD_DIM), F32),
        ),
        scratch_shapes=[
            pltpu.VMEM((1, PROMPT_TILE + SUBLANES, D_CONV), F32),
            pltpu.VMEM((RET_HEADS, RET_HEAD_DIM, RET_HEAD_DIM), F32),
            pltpu.VMEM((2, PROMPT_TILE, D_MODEL), F32),
            pltpu.VMEM((PROMPT_TILE, D_MODEL), BF16),
            pltpu.VMEM((PROMPT_TILE, 3 * D_CONV), F32),
        ],
        compiler_params=pltpu.CompilerParams(
            dimension_semantics=("arbitrary",),
            vmem_limit_bytes=VMEM_LIMIT_BYTES),
        name="prompt_step",
    )(x_prompt, x_prompt, cos_p, sin_p, meta_hist, meta_state, *params)

    ns = SAMPLE_STREAMS
    y_sample, conv_s, ret_s = pl.pallas_call(
        _sample_kernel,
        grid=(dec_batch // ns,),
        in_specs=[
            pl.BlockSpec((ns, dec_seq, D_MODEL), lambda b: (b, 0, 0)),
            _const_spec(cos_s.shape),
            _const_spec(sin_s.shape),
            pl.BlockSpec((ns, CONV_WIDTH - 1, D_CONV), lambda b: (b, 0, 0)),
            pl.BlockSpec((ns, RET_HEADS, RET_HEAD_DIM, RET_HEAD_DIM), lambda b: (b, 0, 0, 0)),
        ] + param_specs,
        out_specs=(
            pl.BlockSpec((ns, dec_seq, D_MODEL), lambda b: (b, 0, 0)),
            pl.BlockSpec((ns, CONV_WIDTH - 1, D_CONV), lambda b: (b, 0, 0)),
            pl.BlockSpec((ns, RET_HEADS, RET_HEAD_DIM, RET_HEAD_DIM), lambda b: (b, 0, 0, 0)),
        ),
        out_shape=(
            jax.ShapeDtypeStruct((dec_batch, dec_seq, D_MODEL), F32),
            jax.ShapeDtypeStruct((dec_batch, CONV_WIDTH - 1, D_CONV), F32),
            jax.ShapeDtypeStruct((dec_batch, RET_HEADS, RET_HEAD_DIM, RET_HEAD_DIM), F32),
        ),
        scratch_shapes=[pltpu.VMEM((ns, dec_seq + SUBLANES, D_CONV), F32)],
        compiler_params=pltpu.CompilerParams(
            dimension_semantics=("arbitrary",),
            vmem_limit_bytes=VMEM_LIMIT_BYTES),
        name="sample_step",
    )(x_sample, cos_s, sin_s, state_conv[0], state_ret[0], *params)

    return (y_prompt, y_sample, conv_p[None], ret_p[None], conv_s[None], ret_s[None])
```

```python
import functools
import math

import jax
import jax.numpy as jnp
import numpy as np
from jax import lax
from jax.experimental import pallas as pl
from jax.experimental.pallas import tpu as pltpu

D_MODEL = 1024
N_META = 16
PAST_LEN = 2048
D_CONV = 512
CONV_WIDTH = 3
RET_HEADS = 4
RET_HEAD_DIM = 128
D_RET = RET_HEADS * RET_HEAD_DIM
D_FF = 4 * D_MODEL
ROPE_BASE = 10000.0
NORM_EPS = 1e-6
GN_EPS = 1e-5
K_SCALE = RET_HEAD_DIM ** -0.5
LOG_GAMMA = tuple(math.log(1.0 - 2.0 ** (-5.0 - h)) for h in range(RET_HEADS))

SUBLANES = 8
VMEM_LIMIT_BYTES = 58 * 1024 * 1024

PROMPT_TILE = 512
PROMPT_CHUNK = 256
SAMPLE_STREAMS = 8
FF_BLOCK = 1024

BF16 = jnp.bfloat16
F32 = jnp.float32


def _dot(a, b):
    return jnp.dot(a, b, preferred_element_type=F32)


def _rmsnorm(x, w):
    ms = jnp.mean(x * x, axis=-1, keepdims=True)
    return x * lax.rsqrt(ms + NORM_EPS) * w


def _rotate(x, cos_t, sin_t):
    return x * cos_t + pltpu.roll(x, RET_HEAD_DIM // 2, 1) * sin_t


def _decay_tables(chunk):
    row = lax.broadcasted_iota(jnp.int32, (chunk, chunk), 0)
    col = lax.broadcasted_iota(jnp.int32, (chunk, chunk), 1)
    diff = (row - col).astype(F32)
    rowv = lax.broadcasted_iota(jnp.int32, (chunk, RET_HEAD_DIM), 0).astype(F32)
    tabs = []
    for lg in LOG_GAMMA:
        inner = jnp.where(diff >= 0.0, jnp.exp(jnp.maximum(diff, 0.0) * lg), 0.0) * K_SCALE
        cross = jnp.exp((rowv + 1.0) * lg)
        kdec = jnp.exp((chunk - 1.0 - rowv) * lg) * K_SCALE
        tabs.append((inner, cross, kdec, math.exp(chunk * lg)))
    return tabs


def _mixer_inputs(x, norm_w, w_in_ref, cos_t, sin_t):
    hn = _rmsnorm(x, norm_w).astype(BF16)
    pc = _dot(hn, w_in_ref[:, : 3 * D_CONV])
    xt = pc[:, :D_CONV]
    b_gate = pc[:, D_CONV: 2 * D_CONV]
    c_gate = pc[:, 2 * D_CONV:]
    u = c_gate * xt
    pr = _dot(hn, w_in_ref[:, 3 * D_CONV:])
    q, k, v, g = (pr[:, i * D_RET:(i + 1) * D_RET] for i in range(4))
    qh, kh, vh = [], [], []
    for h in range(RET_HEADS):
        sl = slice(h * RET_HEAD_DIM, (h + 1) * RET_HEAD_DIM)
        qh.append(_rotate(q[:, sl], cos_t, sin_t).astype(BF16))
        kh.append(_rotate(k[:, sl], cos_t, sin_t))
        vh.append(v[:, sl].astype(BF16))
    return u, b_gate, qh, kh, vh, g


def _retention_chunk(qc, kc, vc, s, tabs):
    inner_decay, cross_decay, k_decay, gamma_l = tabs
    scores = lax.dot_general(qc, kc.astype(BF16), (((1,), (1,)), ((), ())),
                             preferred_element_type=F32)
    p = (scores * inner_decay).astype(BF16)
    o = _dot(p, vc) + _dot(qc, s.astype(BF16)) * cross_decay
    kd = (kc * k_decay).astype(BF16)
    s_new = gamma_l * s + lax.dot_general(kd, vc, (((0,), (0,)), ((), ())),
                                          preferred_element_type=F32)
    return o, s_new


def _retention(qh, kh, vh, states, n_streams, seg, chunk, tabs):
    new_states = []
    rows = []
    for n in range(n_streams):
        st = list(states[n])
        chunk_rows = []
        for c in range(seg // chunk):
            lo = n * seg + c * chunk
            heads = []
            for h in range(RET_HEADS):
                o, st[h] = _retention_chunk(qh[h][lo:lo + chunk], kh[h][lo:lo + chunk],
                                            vh[h][lo:lo + chunk], st[h], tabs[h])
                heads.append(o)
            chunk_rows.append(heads)
        rows.extend(chunk_rows)
        new_states.append(st)
    o_heads = [jnp.concatenate([r[h] for r in rows], axis=0) if len(rows) > 1 else rows[0][h]
               for h in range(RET_HEADS)]
    return o_heads, new_states


def _group_norm_gate(o_heads, g, gn_w):
    outs = []
    for h in range(RET_HEADS):
        o = o_heads[h]
        mu = jnp.mean(o, axis=-1, keepdims=True)
        d = o - mu
        var = jnp.mean(d * d, axis=-1, keepdims=True)
        sl = slice(h * RET_HEAD_DIM, (h + 1) * RET_HEAD_DIM)
        gh = g[:, sl]
        outs.append((gh * jax.nn.sigmoid(gh)) * (d * lax.rsqrt(var + GN_EPS) * gn_w[:, sl]))
    return jnp.concatenate(outs, axis=1).astype(BF16)


def _ffn_up(n2, w_up_ref, j):
    up = jnp.maximum(_dot(n2, w_up_ref[:, j * FF_BLOCK:(j + 1) * FF_BLOCK]), 0.0)
    return (up * up).astype(BF16)


def _ffn_down(hid, w_down_ref, j):
    return _dot(hid, w_down_ref[j * FF_BLOCK:(j + 1) * FF_BLOCK, :])


def _ffn_and_final(h1, norm_ffn_w, w_up_ref, w_down_ref, norm_final_w):
    n2 = _rmsnorm(h1, norm_ffn_w).astype(BF16)
    f = None
    for j in range(D_FF // FF_BLOCK):
        d = _ffn_down(_ffn_up(n2, w_up_ref, j), w_down_ref, j)
        f = d if f is None else f + d
    return _rmsnorm(h1 + f, norm_final_w)


def _conv_group(u, b_gate, conv_w, ubuf_ref, n_streams, seg):
    ubuf_ref[:, SUBLANES:SUBLANES + seg, :] = u.reshape(n_streams, seg, D_CONV)
    u1 = ubuf_ref[:, SUBLANES - 1:SUBLANES - 1 + seg, :].reshape(n_streams * seg, D_CONV)
    u2 = ubuf_ref[:, SUBLANES - 2:SUBLANES - 2 + seg, :].reshape(n_streams * seg, D_CONV)
    conv_y = conv_w[0:1, :] * u2 + conv_w[1:2, :] * u1 + conv_w[2:3, :] * u
    new_hist = ubuf_ref[:, SUBLANES + seg - 2:SUBLANES + seg, :]
    return (b_gate * conv_y).astype(BF16), new_hist


def _layer(x, hist_ref_init, states, n_streams, seg, chunk, ubuf_ref, cos_t, sin_t,
           norm_mix_w, w_in_ref, conv_w, gn_w, w_out_ref, norm_ffn_w, w_up_ref, w_down_ref,
           norm_final_w):
    u, b_gate, qh, kh, vh, g = _mixer_inputs(x, norm_mix_w, w_in_ref, cos_t, sin_t)
    if hist_ref_init is not None:
        ubuf_ref[:, SUBLANES - 2:SUBLANES, :] = hist_ref_init
    conv_out, new_hist = _conv_group(u, b_gate, conv_w, ubuf_ref, n_streams, seg)
    tabs = _decay_tables(chunk)
    o_heads, new_states = _retention(qh, kh, vh, states, n_streams, seg, chunk, tabs)
    ret_out = _group_norm_gate(o_heads, g, gn_w)
    h1 = x + (_dot(conv_out, w_out_ref[:D_CONV, :]) + _dot(ret_out, w_out_ref[D_CONV:, :]))
    y = _ffn_and_final(h1, norm_ffn_w, w_up_ref, w_down_ref, norm_final_w)
    return y, new_hist, new_states


def _meta_kernel(meta_ref, cos_ref, sin_ref, norm_mix_ref, w_in_ref, hist_out_ref, state_out_ref):
    u, _, qh, kh, vh, _ = _mixer_inputs(meta_ref[...], norm_mix_ref[...], w_in_ref,
                                        cos_ref[...], sin_ref[...])
    hist_out_ref[...] = u[N_META - 2:N_META, :]
    tabs = _decay_tables(N_META)
    for h in range(RET_HEADS):
        _, _, k_decay, _ = tabs[h]
        kd = (kh[h] * k_decay).astype(BF16)
        state_out_ref[h] = lax.dot_general(kd, vh[h], (((0,), (0,)), ((), ())),
                                           preferred_element_type=F32)


def _prompt_step(x_ref, cos_ref, sin_ref, norm_mix_ref, w_in_ref, conv_w_ref, gn_w_ref, w_out_ref,
                 norm_ffn_ref, w_up_ref, w_down_ref, norm_final_ref,
                 y_ref, hist_out_ref, state_out_ref, ubuf_ref, state_ref, h1_ref,
                 *, mixer, ffn):
    n_chunks = PROMPT_TILE // PROMPT_CHUNK
    n_halves = 2 * (D_FF // FF_BLOCK)
    assert n_chunks * RET_HEADS == n_halves
    if ffn:
        h1_prev = h1_ref[...]
        n2 = _rmsnorm(h1_prev, norm_ffn_ref[...]).astype(BF16)
    carry = {"f": None, "hid": None}

    def ffn_half(k):
        if not ffn:
            return
        if k % 2 == 0:
            carry["hid"] = _ffn_up(n2, w_up_ref, k // 2)
        else:
            d = _ffn_down(carry["hid"], w_down_ref, k // 2)
            carry["f"] = d if carry["f"] is None else carry["f"] + d

    if mixer:
        x = x_ref[0]
        u, b_gate, qh, kh, vh, g = _mixer_inputs(x, norm_mix_ref[...], w_in_ref,
                                                 cos_ref[...], sin_ref[...])
        conv_out, new_hist = _conv_group(u, b_gate, conv_w_ref[...], ubuf_ref, 1, PROMPT_TILE)
        tabs = _decay_tables(PROMPT_CHUNK)
        st = [state_ref[h] for h in range(RET_HEADS)]
        o_chunks = [[None] * n_chunks for _ in range(RET_HEADS)]

    for c in range(n_chunks):
        rows = slice(c * PROMPT_CHUNK, (c + 1) * PROMPT_CHUNK)
        for h in range(RET_HEADS):
            if mixer:
                o_chunks[h][c], st[h] = _retention_chunk(qh[h][rows], kh[h][rows], vh[h][rows],
                                                         st[h], tabs[h])
            ffn_half(c * RET_HEADS + h)
    if ffn:
        y_ref[0] = _rmsnorm(h1_prev + carry["f"], norm_final_ref[...])

    if mixer:
        o_heads = [jnp.concatenate(o_chunks[h], axis=0) for h in range(RET_HEADS)]
        ret_out = _group_norm_gate(o_heads, g, gn_w_ref[...])
        h1_ref[...] = x + (_dot(conv_out, w_out_ref[:D_CONV, :])
                           + _dot(ret_out, w_out_ref[D_CONV:, :]))
        ubuf_ref[:, SUBLANES - 2:SUBLANES, :] = new_hist
        hist_out_ref[...] = new_hist
        for h in range(RET_HEADS):
            state_ref[h] = st[h]
            state_out_ref[0, h] = st[h]


def _prompt_kernel(x_ref, cos_ref, sin_ref, hist0_ref, state0_ref, *refs, tiles_per_stream, n_tiles):
    t = pl.program_id(0)

    @pl.when(t % tiles_per_stream == 0)
    def _():
        ubuf_ref, state_ref = refs[-3], refs[-2]
        ubuf_ref[0, SUBLANES - 2:SUBLANES, :] = hist0_ref[...]
        state_ref[...] = state0_ref[...]

    step = functools.partial(_prompt_step, x_ref, cos_ref, sin_ref, *refs)
    pl.when(t == 0)(functools.partial(step, mixer=True, ffn=False))
    pl.when(jnp.logical_and(t > 0, t < n_tiles))(functools.partial(step, mixer=True, ffn=True))
    pl.when(t == n_tiles)(functools.partial(step, mixer=False, ffn=True))


def _sample_kernel(x_ref, cos_ref, sin_ref, hist_ref, state_in_ref,
                   norm_mix_ref, w_in_ref, conv_w_ref, gn_w_ref, w_out_ref, norm_ffn_ref,
                   w_up_ref, w_down_ref, norm_final_ref,
                   y_ref, hist_out_ref, state_out_ref, ubuf_ref):
    n, seg = SAMPLE_STREAMS, x_ref.shape[1]
    states = [[state_in_ref[i, h] for h in range(RET_HEADS)] for i in range(n)]
    cos_t = jnp.concatenate([cos_ref[...]] * n, axis=0)
    sin_t = jnp.concatenate([sin_ref[...]] * n, axis=0)
    y, new_hist, new_states = _layer(
        x_ref[...].reshape(n * seg, D_MODEL), hist_ref[...], states, n, seg, seg, ubuf_ref,
        cos_t, sin_t, norm_mix_ref[...], w_in_ref, conv_w_ref[...],
        gn_w_ref[...], w_out_ref, norm_ffn_ref[...], w_up_ref, w_down_ref, norm_final_ref[...])
    y_ref[...] = y.reshape(n, seg, D_MODEL)
    hist_out_ref[...] = new_hist
    for i in range(n):
        for h in range(RET_HEADS):
            state_out_ref[i, h] = new_states[i][h]


def _rope_tables(first_pos, n):
    half = RET_HEAD_DIM // 2
    inv_freq = ROPE_BASE ** (-np.arange(half, dtype=np.float64) / half)
    ang = (first_pos + np.arange(n, dtype=np.float64))[:, None] * inv_freq[None, :]
    cos, sin = np.cos(ang), np.sin(ang)
    return (jnp.asarray(np.concatenate([cos, cos], axis=-1), dtype=F32),
            jnp.asarray(np.concatenate([-sin, sin], axis=-1), dtype=F32))


def _const_spec(shape):
    nd = len(shape)
    return pl.BlockSpec(shape, lambda *_: (0,) * nd, pipeline_mode=pl.Buffered(1))


def kernel(x_prompt, x_sample, state_conv, state_ret, meta_tokens, norm_mix_w, w_in, conv_w,
           ret_gn_w, w_out, norm_ffn_w, w_up, w_down, norm_final_w):
    depth = w_in.shape[0]
    assert depth == 1, "single-layer step"
    batch, seq, _ = x_prompt.shape
    dec_batch, dec_seq, _ = x_sample.shape
    assert seq % PROMPT_TILE == 0 and dec_batch % SAMPLE_STREAMS == 0

    w_in_b, w_out_b = w_in[0].astype(BF16), w_out[0].astype(BF16)
    w_up_b, w_down_b = w_up[0].astype(BF16), w_down[0].astype(BF16)
    norm_mix, norm_ffn = norm_mix_w[0][None], norm_ffn_w[0][None]
    norm_final = norm_final_w[None]
    gn_w, cw = ret_gn_w[0][None], conv_w[0]

    cos_m, sin_m = _rope_tables(-N_META, N_META)
    cos_p, sin_p = _rope_tables(0, seq)
    cos_s, sin_s = _rope_tables(PAST_LEN, dec_seq)

    params = (norm_mix, w_in_b, cw, gn_w, w_out_b, norm_ffn, w_up_b, w_down_b, norm_final)
    param_specs = [_const_spec(p.shape) for p in params]
    d_in = w_in_b.shape[1]

    meta_hist, meta_state = pl.pallas_call(
        _meta_kernel,
        out_shape=(jax.ShapeDtypeStruct((CONV_WIDTH - 1, D_CONV), F32),
                   jax.ShapeDtypeStruct((RET_HEADS, RET_HEAD_DIM, RET_HEAD_DIM), F32)),
        compiler_params=pltpu.CompilerParams(vmem_limit_bytes=VMEM_LIMIT_BYTES),
        name="meta_state",
    )(meta_tokens, cos_m, sin_m, norm_mix, w_in_b)

    tiles_per_stream = seq // PROMPT_TILE
    n_tiles = batch * tiles_per_stream

    def cur(t):
        return jnp.minimum(t, n_tiles - 1)

    def prev(t):
        return jnp.maximum(t - 1, 0)

    y_prompt, conv_p, ret_p = pl.pallas_call(
        functools.partial(_prompt_kernel, tiles_per_stream=tiles_per_stream, n_tiles=n_tiles),
        grid=(n_tiles + 1,),
        in_specs=[
            pl.BlockSpec((1, PROMPT_TILE, D_MODEL),
                         lambda t: (cur(t) // tiles_per_stream, cur(t) % tiles_per_stream, 0)),
            pl.BlockSpec((PROMPT_TILE, RET_HEAD_DIM), lambda t: (cur(t) % tiles_per_stream, 0)),
            pl.BlockSpec((PROMPT_TILE, RET_HEAD_DIM), lambda t: (cur(t) % tiles_per_stream, 0)),
            _const_spec(meta_hist.shape),
            _const_spec(meta_state.shape),
        ] + param_specs,
        out_specs=(
            pl.BlockSpec((1, PROMPT_TILE, D_MODEL),
                         lambda t: (prev(t) // tiles_per_stream, prev(t) % tiles_per_stream, 0)),
            pl.BlockSpec((1, CONV_WIDTH - 1, D_CONV), lambda t: (cur(t) // tiles_per_stream, 0, 0)),
            pl.BlockSpec((1, RET_HEADS, RET_HEAD_DIM, RET_HEAD_DIM),
                         lambda t: (cur(t) // tiles_per_stream, 0, 0, 0)),
        ),
        out_shape=(
            jax.ShapeDtypeStruct((batch, seq, D_MODEL), F32),
            jax.ShapeDtypeStruct((batch, CONV_WIDTH - 1, D_CONV), F32),
            jax.ShapeDtypeStruct((batch, RET_HEADS, RET_HEAD_DIM, RET_HEAD_DIM), F32),
        ),
        scratch_shapes=[
            pltpu.VMEM((1, PROMPT_TILE + SUBLANES, D_CONV), F32),
            pltpu.VMEM((RET_HEADS, RET_HEAD_DIM, RET_HEAD_DIM), F32),
            pltpu.VMEM((PROMPT_TILE, D_MODEL), F32),
        ],
        compiler_params=pltpu.CompilerParams(
            dimension_semantics=("arbitrary",),
            vmem_limit_bytes=VMEM_LIMIT_BYTES),
        name="prompt_step",
    )(x_prompt, cos_p, sin_p, meta_hist, meta_state, *params)

    ns = SAMPLE_STREAMS
    y_sample, conv_s, ret_s = pl.pallas_call(
        _sample_kernel,
        grid=(dec_batch // ns,),
        in_specs=[
            pl.BlockSpec((ns, dec_seq, D_MODEL), lambda b: (b, 0, 0)),
            _const_spec(cos_s.shape),
            _const_spec(sin_s.shape),
            pl.BlockSpec((ns, CONV_WIDTH - 1, D_CONV), lambda b: (b, 0, 0)),
            pl.BlockSpec((ns, RET_HEADS, RET_HEAD_DIM, RET_HEAD_DIM), lambda b: (b, 0, 0, 0)),
        ] + param_specs,
        out_specs=(
            pl.BlockSpec((ns, dec_seq, D_MODEL), lambda b: (b, 0, 0)),
            pl.BlockSpec((ns, CONV_WIDTH - 1, D_CONV), lambda b: (b, 0, 0)),
            pl.BlockSpec((ns, RET_HEADS, RET_HEAD_DIM, RET_HEAD_DIM), lambda b: (b, 0, 0, 0)),
        ),
        out_shape=(
            jax.ShapeDtypeStruct((dec_batch, dec_seq, D_MODEL), F32),
            jax.ShapeDtypeStruct((dec_batch, CONV_WIDTH - 1, D_CONV), F32),
            jax.ShapeDtypeStruct((dec_batch, RET_HEADS, RET_HEAD_DIM, RET_HEAD_DIM), F32),
        ),
        scratch_shapes=[pltpu.VMEM((ns, dec_seq + SUBLANES, D_CONV), F32)],
        compiler_params=pltpu.CompilerParams(
            dimension_semantics=("arbitrary",),
            vmem_limit_bytes=VMEM_LIMIT_BYTES),
        name="sample_step",
    )(x_sample, cos_s, sin_s, state_conv[0], state_ret[0], *params)

    return (y_prompt, y_sample, conv_p[None], ret_p[None], conv_s[None], ret_s[None])
```

```python
import functools
import math

import jax
import jax.numpy as jnp
import numpy as np
from jax import lax
from jax.experimental import pallas as pl
from jax.experimental.pallas import tpu as pltpu

D_MODEL = 1024
N_META = 16
PAST_LEN = 2048
D_CONV = 512
CONV_WIDTH = 3
RET_HEADS = 4
RET_HEAD_DIM = 128
D_RET = RET_HEADS * RET_HEAD_DIM
D_FF = 4 * D_MODEL
ROPE_BASE = 10000.0
NORM_EPS = 1e-6
GN_EPS = 1e-5
K_SCALE = RET_HEAD_DIM ** -0.5
LOG_GAMMA = tuple(math.log(1.0 - 2.0 ** (-5.0 - h)) for h in range(RET_HEADS))

SUBLANES = 8
VMEM_LIMIT_BYTES = 58 * 1024 * 1024

PROMPT_TILE = 512
PROMPT_CHUNK = 256
SAMPLE_STREAMS = 8
FF_BLOCK = 1024

BF16 = jnp.bfloat16
F32 = jnp.float32


def _dot(a, b):
    return jnp.dot(a, b, preferred_element_type=F32)


def _rmsnorm(x, w):
    ms = jnp.mean(x * x, axis=-1, keepdims=True)
    return x * lax.rsqrt(ms + NORM_EPS) * w


def _rotate(x, cos_t, sin_t):
    return x * cos_t + pltpu.roll(x, RET_HEAD_DIM // 2, 1) * sin_t


def _decay_tables(chunk):
    row = lax.broadcasted_iota(jnp.int32, (chunk, chunk), 0)
    col = lax.broadcasted_iota(jnp.int32, (chunk, chunk), 1)
    diff = (row - col).astype(F32)
    rowv = lax.broadcasted_iota(jnp.int32, (chunk, RET_HEAD_DIM), 0).astype(F32)
    tabs = []
    for lg in LOG_GAMMA:
        inner = jnp.where(diff >= 0.0, jnp.exp(jnp.maximum(diff, 0.0) * lg), 0.0) * K_SCALE
        cross = jnp.exp((rowv + 1.0) * lg)
        kdec = jnp.exp((chunk - 1.0 - rowv) * lg) * K_SCALE
        tabs.append((inner, cross, kdec, math.exp(chunk * lg)))
    return tabs


def _mixer_inputs(x, norm_w, w_in_ref, cos_t, sin_t):
    hn = _rmsnorm(x, norm_w).astype(BF16)
    pc = _dot(hn, w_in_ref[:, : 3 * D_CONV])
    xt = pc[:, :D_CONV]
    b_gate = pc[:, D_CONV: 2 * D_CONV]
    c_gate = pc[:, 2 * D_CONV:]
    u = c_gate * xt
    pr = _dot(hn, w_in_ref[:, 3 * D_CONV:])
    q, k, v, g = (pr[:, i * D_RET:(i + 1) * D_RET] for i in range(4))
    qh, kh, vh = [], [], []
    for h in range(RET_HEADS):
        sl = slice(h * RET_HEAD_DIM, (h + 1) * RET_HEAD_DIM)
        qh.append(_rotate(q[:, sl], cos_t, sin_t).astype(BF16))
        kh.append(_rotate(k[:, sl], cos_t, sin_t))
        vh.append(v[:, sl].astype(BF16))
    return u, b_gate, qh, kh, vh, g


def _retention_chunk(qc, kc, vc, s, tabs):
    inner_decay, cross_decay, k_decay, gamma_l = tabs
    scores = lax.dot_general(qc, kc.astype(BF16), (((1,), (1,)), ((), ())),
                             preferred_element_type=F32)
    p = (scores * inner_decay).astype(BF16)
    o = _dot(p, vc) + _dot(qc, s.astype(BF16)) * cross_decay
    kd = (kc * k_decay).astype(BF16)
    s_new = gamma_l * s + lax.dot_general(kd, vc, (((0,), (0,)), ((), ())),
                                          preferred_element_type=F32)
    return o, s_new


def _retention(qh, kh, vh, states, n_streams, seg, chunk, tabs):
    new_states = []
    rows = []
    for n in range(n_streams):
        st = list(states[n])
        chunk_rows = []
        for c in range(seg // chunk):
            lo = n * seg + c * chunk
            heads = []
            for h in range(RET_HEADS):
                o, st[h] = _retention_chunk(qh[h][lo:lo + chunk], kh[h][lo:lo + chunk],
                                            vh[h][lo:lo + chunk], st[h], tabs[h])
                heads.append(o)
            chunk_rows.append(heads)
        rows.extend(chunk_rows)
        new_states.append(st)
    o_heads = [jnp.concatenate([r[h] for r in rows], axis=0) if len(rows) > 1 else rows[0][h]
               for h in range(RET_HEADS)]
    return o_heads, new_states


def _group_norm_gate(o_heads, g, gn_w):
    outs = []
    for h in range(RET_HEADS):
        o = o_heads[h]
        mu = jnp.mean(o, axis=-1, keepdims=True)
        d = o - mu
        var = jnp.mean(d * d, axis=-1, keepdims=True)
        sl = slice(h * RET_HEAD_DIM, (h + 1) * RET_HEAD_DIM)
        gh = g[:, sl]
        outs.append((gh * jax.nn.sigmoid(gh)) * (d * lax.rsqrt(var + GN_EPS) * gn_w[:, sl]))
    return jnp.concatenate(outs, axis=1).astype(BF16)


def _ffn_up(n2, w_up_ref, j):
    up = jnp.maximum(_dot(n2, w_up_ref[:, j * FF_BLOCK:(j + 1) * FF_BLOCK]), 0.0)
    return (up * up).astype(BF16)


def _ffn_down(hid, w_down_ref, j):
    return _dot(hid, w_down_ref[j * FF_BLOCK:(j + 1) * FF_BLOCK, :])


def _ffn_and_final(h1, norm_ffn_w, w_up_ref, w_down_ref, norm_final_w):
    n2 = _rmsnorm(h1, norm_ffn_w).astype(BF16)
    f = None
    for j in range(D_FF // FF_BLOCK):
        d = _ffn_down(_ffn_up(n2, w_up_ref, j), w_down_ref, j)
        f = d if f is None else f + d
    return _rmsnorm(h1 + f, norm_final_w)


def _conv_group(u, b_gate, conv_w, ubuf_ref, n_streams, seg):
    ubuf_ref[:, SUBLANES:SUBLANES + seg, :] = u.reshape(n_streams, seg, D_CONV)
    u1 = ubuf_ref[:, SUBLANES - 1:SUBLANES - 1 + seg, :].reshape(n_streams * seg, D_CONV)
    u2 = ubuf_ref[:, SUBLANES - 2:SUBLANES - 2 + seg, :].reshape(n_streams * seg, D_CONV)
    conv_y = conv_w[0:1, :] * u2 + conv_w[1:2, :] * u1 + conv_w[2:3, :] * u
    new_hist = ubuf_ref[:, SUBLANES + seg - 2:SUBLANES + seg, :]
    return (b_gate * conv_y).astype(BF16), new_hist


def _layer(x, hist_ref_init, states, n_streams, seg, chunk, ubuf_ref, cos_t, sin_t,
           norm_mix_w, w_in_ref, conv_w, gn_w, w_out_ref, norm_ffn_w, w_up_ref, w_down_ref,
           norm_final_w):
    u, b_gate, qh, kh, vh, g = _mixer_inputs(x, norm_mix_w, w_in_ref, cos_t, sin_t)
    if hist_ref_init is not None:
        ubuf_ref[:, SUBLANES - 2:SUBLANES, :] = hist_ref_init
    conv_out, new_hist = _conv_group(u, b_gate, conv_w, ubuf_ref, n_streams, seg)
    tabs = _decay_tables(chunk)
    o_heads, new_states = _retention(qh, kh, vh, states, n_streams, seg, chunk, tabs)
    ret_out = _group_norm_gate(o_heads, g, gn_w)
    h1 = x + (_dot(conv_out, w_out_ref[:D_CONV, :]) + _dot(ret_out, w_out_ref[D_CONV:, :]))
    y = _ffn_and_final(h1, norm_ffn_w, w_up_ref, w_down_ref, norm_final_w)
    return y, new_hist, new_states


def _meta_kernel(meta_ref, cos_ref, sin_ref, norm_mix_ref, w_in_ref, hist_out_ref, state_out_ref):
    u, _, qh, kh, vh, _ = _mixer_inputs(meta_ref[...], norm_mix_ref[...], w_in_ref,
                                        cos_ref[...], sin_ref[...])
    hist_out_ref[...] = u[N_META - 2:N_META, :]
    tabs = _decay_tables(N_META)
    for h in range(RET_HEADS):
        _, _, k_decay, _ = tabs[h]
        kd = (kh[h] * k_decay).astype(BF16)
        state_out_ref[h] = lax.dot_general(kd, vh[h], (((0,), (0,)), ((), ())),
                                           preferred_element_type=F32)


def _prompt_step(x_ref, cos_ref, sin_ref, norm_mix_ref, w_in_ref, conv_w_ref, gn_w_ref, w_out_ref,
                 norm_ffn_ref, w_up_ref, w_down_ref, norm_final_ref,
                 y_ref, hist_out_ref, state_out_ref, ubuf_ref, state_ref, h1_ref,
                 inner_ref, cross_ref, kdec_ref, *, keep_outputs):
    n_chunks = PROMPT_TILE // PROMPT_CHUNK
    n_blocks = D_FF // FF_BLOCK
    assert n_chunks * RET_HEADS == 2 * n_blocks
    half = FF_BLOCK // 2

    x = x_ref[0]
    u, b_gate, qh, kh, vh, g = _mixer_inputs(x, norm_mix_ref[...], w_in_ref,
                                             cos_ref[...], sin_ref[...])
    conv_out, new_hist = _conv_group(u, b_gate, conv_w_ref[...], ubuf_ref, 1, PROMPT_TILE)
    tabs = [(inner_ref[h], cross_ref[h], kdec_ref[h], math.exp(PROMPT_CHUNK * LOG_GAMMA[h]))
            for h in range(RET_HEADS)]
    st = [state_ref[h] for h in range(RET_HEADS)]
    o_chunks = [[None] * n_chunks for _ in range(RET_HEADS)]
    h1_prev = h1_ref[...]
    n2 = _rmsnorm(h1_prev, norm_ffn_ref[...]).astype(BF16)
    carry = {"f": None, "hid": None, "hid_lo": None}

    def up_lo():
        up = jnp.maximum(_dot(n2, w_up_ref[:, :half]), 0.0)
        carry["hid_lo"] = (up * up).astype(BF16)

    def up_hi():
        up = jnp.maximum(_dot(n2, w_up_ref[:, half:FF_BLOCK]), 0.0)
        carry["hid"] = jnp.concatenate([carry["hid_lo"], (up * up).astype(BF16)], axis=1)

    def up(j):
        carry["hid"] = _ffn_up(n2, w_up_ref, j)

    def down(j):
        d = _ffn_down(carry["hid"], w_down_ref, j)
        carry["f"] = d if carry["f"] is None else carry["f"] + d

    items = [up_lo, up_hi, functools.partial(down, 0)]
    for j in range(1, n_blocks):
        items += [functools.partial(up, j), functools.partial(down, j)]
    assert len(items) == n_chunks * RET_HEADS + 1
    items.reverse()

    items.pop()()
    for c in range(n_chunks):
        rows = slice(c * PROMPT_CHUNK, (c + 1) * PROMPT_CHUNK)
        for h in range(RET_HEADS):
            o_chunks[h][c], st[h] = _retention_chunk(qh[h][rows], kh[h][rows], vh[h][rows],
                                                     st[h], tabs[h])
            items.pop()()
    y_ref[0] = _rmsnorm(h1_prev + carry["f"], norm_final_ref[...])

    o_heads = [jnp.concatenate(o_chunks[h], axis=0) for h in range(RET_HEADS)]
    ret_out = _group_norm_gate(o_heads, g, gn_w_ref[...])
    h1_ref[...] = x + (_dot(conv_out, w_out_ref[:D_CONV, :]) + _dot(ret_out, w_out_ref[D_CONV:, :]))
    ubuf_ref[:, SUBLANES - 2:SUBLANES, :] = new_hist
    for h in range(RET_HEADS):
        state_ref[h] = st[h]

    @pl.when(keep_outputs)
    def _():
        hist_out_ref[...] = new_hist
        for h in range(RET_HEADS):
            state_out_ref[0, h] = st[h]


def _prompt_kernel(x_ref, cos_ref, sin_ref, hist0_ref, state0_ref, *refs, tiles_per_stream, n_tiles):
    t = pl.program_id(0)
    ubuf_ref, state_ref, h1_ref, inner_ref, cross_ref, kdec_ref = refs[-6:]

    @pl.when(t == 0)
    def _():
        h1_ref[...] = jnp.zeros_like(h1_ref)
        for h, (inner, cross, kdec, _) in enumerate(_decay_tables(PROMPT_CHUNK)):
            inner_ref[h] = inner
            cross_ref[h] = cross
            kdec_ref[h] = kdec

    @pl.when(t % tiles_per_stream == 0)
    def _():
        ubuf_ref[0, SUBLANES - 2:SUBLANES, :] = hist0_ref[...]
        state_ref[...] = state0_ref[...]

    _prompt_step(x_ref, cos_ref, sin_ref, *refs, keep_outputs=t < n_tiles)


def _sample_kernel(x_ref, cos_ref, sin_ref, hist_ref, state_in_ref,
                   norm_mix_ref, w_in_ref, conv_w_ref, gn_w_ref, w_out_ref, norm_ffn_ref,
                   w_up_ref, w_down_ref, norm_final_ref,
                   y_ref, hist_out_ref, state_out_ref, ubuf_ref):
    n, seg = SAMPLE_STREAMS, x_ref.shape[1]
    states = [[state_in_ref[i, h] for h in range(RET_HEADS)] for i in range(n)]
    cos_t = jnp.concatenate([cos_ref[...]] * n, axis=0)
    sin_t = jnp.concatenate([sin_ref[...]] * n, axis=0)
    y, new_hist, new_states = _layer(
        x_ref[...].reshape(n * seg, D_MODEL), hist_ref[...], states, n, seg, seg, ubuf_ref,
        cos_t, sin_t, norm_mix_ref[...], w_in_ref, conv_w_ref[...],
        gn_w_ref[...], w_out_ref, norm_ffn_ref[...], w_up_ref, w_down_ref, norm_final_ref[...])
    y_ref[...] = y.reshape(n, seg, D_MODEL)
    hist_out_ref[...] = new_hist
    for i in range(n):
        for h in range(RET_HEADS):
            state_out_ref[i, h] = new_states[i][h]


def _rope_tables(first_pos, n):
    half = RET_HEAD_DIM // 2
    inv_freq = ROPE_BASE ** (-np.arange(half, dtype=np.float64) / half)
    ang = (first_pos + np.arange(n, dtype=np.float64))[:, None] * inv_freq[None, :]
    cos, sin = np.cos(ang), np.sin(ang)
    return (jnp.asarray(np.concatenate([cos, cos], axis=-1), dtype=F32),
            jnp.asarray(np.concatenate([-sin, sin], axis=-1), dtype=F32))


def _const_spec(shape):
    nd = len(shape)
    return pl.BlockSpec(shape, lambda *_: (0,) * nd, pipeline_mode=pl.Buffered(1))


def kernel(x_prompt, x_sample, state_conv, state_ret, meta_tokens, norm_mix_w, w_in, conv_w,
           ret_gn_w, w_out, norm_ffn_w, w_up, w_down, norm_final_w):
    depth = w_in.shape[0]
    assert depth == 1, "single-layer step"
    batch, seq, _ = x_prompt.shape
    dec_batch, dec_seq, _ = x_sample.shape
    assert seq % PROMPT_TILE == 0 and dec_batch % SAMPLE_STREAMS == 0

    w_in_b, w_out_b = w_in[0].astype(BF16), w_out[0].astype(BF16)
    w_up_b, w_down_b = w_up[0].astype(BF16), w_down[0].astype(BF16)
    norm_mix, norm_ffn = norm_mix_w[0][None], norm_ffn_w[0][None]
    norm_final = norm_final_w[None]
    gn_w, cw = ret_gn_w[0][None], conv_w[0]

    cos_m, sin_m = _rope_tables(-N_META, N_META)
    cos_p, sin_p = _rope_tables(0, seq)
    cos_s, sin_s = _rope_tables(PAST_LEN, dec_seq)

    params = (norm_mix, w_in_b, cw, gn_w, w_out_b, norm_ffn, w_up_b, w_down_b, norm_final)
    param_specs = [_const_spec(p.shape) for p in params]
    d_in = w_in_b.shape[1]

    meta_hist, meta_state = pl.pallas_call(
        _meta_kernel,
        out_shape=(jax.ShapeDtypeStruct((CONV_WIDTH - 1, D_CONV), F32),
                   jax.ShapeDtypeStruct((RET_HEADS, RET_HEAD_DIM, RET_HEAD_DIM), F32)),
        compiler_params=pltpu.CompilerParams(vmem_limit_bytes=VMEM_LIMIT_BYTES),
        name="meta_state",
    )(meta_tokens, cos_m, sin_m, norm_mix, w_in_b)

    tiles_per_stream = seq // PROMPT_TILE
    n_tiles = batch * tiles_per_stream

    def cur(t):
        return jnp.minimum(t, n_tiles - 1)

    def prev(t):
        return jnp.maximum(t - 1, 0)

    y_prompt, conv_p, ret_p = pl.pallas_call(
        functools.partial(_prompt_kernel, tiles_per_stream=tiles_per_stream, n_tiles=n_tiles),
        grid=(n_tiles + 1,),
        in_specs=[
            pl.BlockSpec((1, PROMPT_TILE, D_MODEL),
                         lambda t: (cur(t) // tiles_per_stream, cur(t) % tiles_per_stream, 0)),
            pl.BlockSpec((PROMPT_TILE, RET_HEAD_DIM), lambda t: (cur(t) % tiles_per_stream, 0)),
            pl.BlockSpec((PROMPT_TILE, RET_HEAD_DIM), lambda t: (cur(t) % tiles_per_stream, 0)),
            _const_spec(meta_hist.shape),
            _const_spec(meta_state.shape),
        ] + param_specs,
        out_specs=(
            pl.BlockSpec((1, PROMPT_TILE, D_MODEL),
                         lambda t: (prev(t) // tiles_per_stream, prev(t) % tiles_per_stream, 0)),
            pl.BlockSpec((1, CONV_WIDTH - 1, D_CONV), lambda t: (cur(t) // tiles_per_stream, 0, 0)),
            pl.BlockSpec((1, RET_HEADS, RET_HEAD_DIM, RET_HEAD_DIM),
                         lambda t: (cur(t) // tiles_per_stream, 0, 0, 0)),
        ),
        out_shape=(
            jax.ShapeDtypeStruct((batch, seq, D_MODEL), F32),
            jax.ShapeDtypeStruct((batch, CONV_WIDTH - 1, D_CONV), F32),
            jax.ShapeDtypeStruct((batch, RET_HEADS, RET_HEAD_DIM, RET_HEAD_DIM), F32),
        ),
        scratch_shapes=[
            pltpu.VMEM((1, PROMPT_TILE + SUBLANES, D_CONV), F32),
            pltpu.VMEM((RET_HEADS, RET_HEAD_DIM, RET_HEAD_DIM), F32),
            pltpu.VMEM((PROMPT_TILE, D_MODEL), F32),
            pltpu.VMEM((RET_HEADS, PROMPT_CHUNK, PROMPT_CHUNK), F32),
            pltpu.VMEM((RET_HEADS, PROMPT_CHUNK, RET_HEAD_DIM), F32),
            pltpu.VMEM((RET_HEADS, PROMPT_CHUNK, RET_HEAD_DIM), F32),
        ],
        compiler_params=pltpu.CompilerParams(
            dimension_semantics=("arbitrary",),
            vmem_limit_bytes=VMEM_LIMIT_BYTES),
        name="prompt_step",
    )(x_prompt, cos_p, sin_p, meta_hist, meta_state, *params)

    ns = SAMPLE_STREAMS
    y_sample, conv_s, ret_s = pl.pallas_call(
        _sample_kernel,
        grid=(dec_batch // ns,),
        in_specs=[
            pl.BlockSpec((ns, dec_seq, D_MODEL), lambda b: (b, 0, 0)),
            _const_spec(cos_s.shape),
            _const_spec(sin_s.shape),
            pl.BlockSpec((ns, CONV_WIDTH - 1, D_CONV), lambda b: (b, 0, 0)),
            pl.BlockSpec((ns, RET_HEADS, RET_HEAD_DIM, RET_HEAD_DIM), lambda b: (b, 0, 0, 0)),
        ] + param_specs,
        out_specs=(
            pl.BlockSpec((ns, dec_seq, D_MODEL), lambda b: (b, 0, 0)),
            pl.BlockSpec((ns, CONV_WIDTH - 1, D_CONV), lambda b: (b, 0, 0)),
            pl.BlockSpec((ns, RET_HEADS, RET_HEAD_DIM, RET_HEAD_DIM), lambda b: (b, 0, 0, 0)),
        ),
        out_shape=(
            jax.ShapeDtypeStruct((dec_batch, dec_seq, D_MODEL), F32),
            jax.ShapeDtypeStruct((dec_batch, CONV_WIDTH - 1, D_CONV), F32),
            jax.ShapeDtypeStruct((dec_batch, RET_HEADS, RET_HEAD_DIM, RET_HEAD_DIM), F32),
        ),
        scratch_shapes=[pltpu.VMEM((ns, dec_seq + SUBLANES, D_CONV), F32)],
        compiler_params=pltpu.CompilerParams(
            dimension_semantics=("arbitrary",),
            vmem_limit_bytes=VMEM_LIMIT_BYTES),
        name="sample_step",
    )(x_sample, cos_s, sin_s, state_conv[0], state_ret[0], *params)

    return (y_prompt, y_sample, conv_p[None], ret_p[None], conv_s[None], ret_s[None])
```

```python
import functools
import math

import jax
import jax.numpy as jnp
import numpy as np
from jax import lax
from jax.experimental import pallas as pl
from jax.experimental.pallas import tpu as pltpu

D_MODEL = 1024
N_META = 16
PAST_LEN = 2048
D_CONV = 512
CONV_WIDTH = 3
RET_HEADS = 4
RET_HEAD_DIM = 128
D_RET = RET_HEADS * RET_HEAD_DIM
D_FF = 4 * D_MODEL
ROPE_BASE = 10000.0
NORM_EPS = 1e-6
GN_EPS = 1e-5
K_SCALE = RET_HEAD_DIM ** -0.5
LOG_GAMMA = tuple(math.log(1.0 - 2.0 ** (-5.0 - h)) for h in range(RET_HEADS))

SUBLANES = 8
VMEM_LIMIT_BYTES = 58 * 1024 * 1024

PROMPT_TILE = 512
PROMPT_CHUNK = 256
SAMPLE_STREAMS = 8
FF_BLOCK = 1024

BF16 = jnp.bfloat16
F32 = jnp.float32


def _dot(a, b):
    return jnp.dot(a, b, preferred_element_type=F32)


def _rmsnorm(x, w):
    ms = jnp.mean(x * x, axis=-1, keepdims=True)
    return x * lax.rsqrt(ms + NORM_EPS) * w


def _rotate(x, cos_t, sin_t):
    return x * cos_t + pltpu.roll(x, RET_HEAD_DIM // 2, 1) * sin_t


def _decay_tables(chunk):
    row = lax.broadcasted_iota(jnp.int32, (chunk, chunk), 0)
    col = lax.broadcasted_iota(jnp.int32, (chunk, chunk), 1)
    diff = (row - col).astype(F32)
    rowv = lax.broadcasted_iota(jnp.int32, (chunk, RET_HEAD_DIM), 0).astype(F32)
    tabs = []
    for lg in LOG_GAMMA:
        inner = jnp.where(diff >= 0.0, jnp.exp(jnp.maximum(diff, 0.0) * lg), 0.0) * K_SCALE
        cross = jnp.exp((rowv + 1.0) * lg)
        kdec = jnp.exp((chunk - 1.0 - rowv) * lg) * K_SCALE
        tabs.append((inner, cross, kdec, math.exp(chunk * lg)))
    return tabs


def _mixer_inputs(x, norm_w, w_in_ref, cos_t, sin_t):
    hn = _rmsnorm(x, norm_w).astype(BF16)
    pc = _dot(hn, w_in_ref[:, : 3 * D_CONV])
    xt = pc[:, :D_CONV]
    b_gate = pc[:, D_CONV: 2 * D_CONV]
    c_gate = pc[:, 2 * D_CONV:]
    u = c_gate * xt
    pr = _dot(hn, w_in_ref[:, 3 * D_CONV:])
    q, k, v, g = (pr[:, i * D_RET:(i + 1) * D_RET] for i in range(4))
    qh, kh, vh = [], [], []
    for h in range(RET_HEADS):
        sl = slice(h * RET_HEAD_DIM, (h + 1) * RET_HEAD_DIM)
        qh.append(_rotate(q[:, sl], cos_t, sin_t).astype(BF16))
        kh.append(_rotate(k[:, sl], cos_t, sin_t))
        vh.append(v[:, sl].astype(BF16))
    return u, b_gate, qh, kh, vh, g


def _retention_scores(qc, kc, vc, s, tabs):
    _, _, k_decay, gamma_l = tabs
    scores = lax.dot_general(qc, kc.astype(BF16), (((1,), (1,)), ((), ())),
                             preferred_element_type=F32)
    kd = (kc * k_decay).astype(BF16)
    s_new = gamma_l * s + lax.dot_general(kd, vc, (((0,), (0,)), ((), ())),
                                          preferred_element_type=F32)
    return scores, s_new


def _retention_output(scores, qc, vc, s, tabs):
    inner_decay, cross_decay, _, _ = tabs
    p = (scores * inner_decay).astype(BF16)
    return _dot(p, vc) + _dot(qc, s.astype(BF16)) * cross_decay


def _retention_chunk(qc, kc, vc, s, tabs):
    scores, s_new = _retention_scores(qc, kc, vc, s, tabs)
    return _retention_output(scores, qc, vc, s, tabs), s_new


def _retention(qh, kh, vh, states, n_streams, seg, chunk, tabs):
    new_states = [list(st) for st in states]
    groups = [(n, c) for n in range(n_streams) for c in range(seg // chunk)]

    def first_stage(n, c):
        lo = n * seg + c * chunk
        return [_retention_scores(qh[h][lo:lo + chunk], kh[h][lo:lo + chunk], vh[h][lo:lo + chunk],
                                  new_states[n][h], tabs[h]) for h in range(RET_HEADS)]

    rows = []
    staged = first_stage(*groups[0])
    for i, (n, c) in enumerate(groups):
        lo = n * seg + c * chunk
        old = list(new_states[n])
        for h in range(RET_HEADS):
            new_states[n][h] = staged[h][1]
        this = staged
        if i + 1 < len(groups):
            staged = first_stage(*groups[i + 1])
        rows.append([_retention_output(this[h][0], qh[h][lo:lo + chunk], vh[h][lo:lo + chunk],
                                       old[h], tabs[h]) for h in range(RET_HEADS)])
    o_heads = [jnp.concatenate([r[h] for r in rows], axis=0) if len(rows) > 1 else rows[0][h]
               for h in range(RET_HEADS)]
    return o_heads, new_states


def _group_norm_gate(o_heads, g, gn_w):
    outs = []
    for h in range(RET_HEADS):
        o = o_heads[h]
        mu = jnp.mean(o, axis=-1, keepdims=True)
        d = o - mu
        var = jnp.mean(d * d, axis=-1, keepdims=True)
        sl = slice(h * RET_HEAD_DIM, (h + 1) * RET_HEAD_DIM)
        gh = g[:, sl]
        outs.append((gh * jax.nn.sigmoid(gh)) * (d * lax.rsqrt(var + GN_EPS) * gn_w[:, sl]))
    return jnp.concatenate(outs, axis=1).astype(BF16)


def _ffn_up(n2, w_up_ref, j):
    up = jnp.maximum(_dot(n2, w_up_ref[:, j * FF_BLOCK:(j + 1) * FF_BLOCK]), 0.0)
    return (up * up).astype(BF16)


def _ffn_down(hid, w_down_ref, j):
    return _dot(hid, w_down_ref[j * FF_BLOCK:(j + 1) * FF_BLOCK, :])


def _ffn_and_final(h1, norm_ffn_w, w_up_ref, w_down_ref, norm_final_w):
    n2 = _rmsnorm(h1, norm_ffn_w).astype(BF16)
    f = None
    for j in range(D_FF // FF_BLOCK):
        d = _ffn_down(_ffn_up(n2, w_up_ref, j), w_down_ref, j)
        f = d if f is None else f + d
    return _rmsnorm(h1 + f, norm_final_w)


def _conv_group(u, b_gate, conv_w, ubuf_ref, n_streams, seg):
    ubuf_ref[:, SUBLANES:SUBLANES + seg, :] = u.reshape(n_streams, seg, D_CONV)
    u1 = ubuf_ref[:, SUBLANES - 1:SUBLANES - 1 + seg, :].reshape(n_streams * seg, D_CONV)
    u2 = ubuf_ref[:, SUBLANES - 2:SUBLANES - 2 + seg, :].reshape(n_streams * seg, D_CONV)
    conv_y = conv_w[0:1, :] * u2 + conv_w[1:2, :] * u1 + conv_w[2:3, :] * u
    new_hist = ubuf_ref[:, SUBLANES + seg - 2:SUBLANES + seg, :]
    return (b_gate * conv_y).astype(BF16), new_hist


def _layer(x, hist_ref_init, states, n_streams, seg, chunk, ubuf_ref, cos_t, sin_t,
           norm_mix_w, w_in_ref, conv_w, gn_w, w_out_ref, norm_ffn_w, w_up_ref, w_down_ref,
           norm_final_w):
    u, b_gate, qh, kh, vh, g = _mixer_inputs(x, norm_mix_w, w_in_ref, cos_t, sin_t)
    if hist_ref_init is not None:
        ubuf_ref[:, SUBLANES - 2:SUBLANES, :] = hist_ref_init
    conv_out, new_hist = _conv_group(u, b_gate, conv_w, ubuf_ref, n_streams, seg)
    tabs = _decay_tables(chunk)
    o_heads, new_states = _retention(qh, kh, vh, states, n_streams, seg, chunk, tabs)
    ret_out = _group_norm_gate(o_heads, g, gn_w)
    h1 = x + (_dot(conv_out, w_out_ref[:D_CONV, :]) + _dot(ret_out, w_out_ref[D_CONV:, :]))
    y = _ffn_and_final(h1, norm_ffn_w, w_up_ref, w_down_ref, norm_final_w)
    return y, new_hist, new_states


def _meta_kernel(meta_ref, cos_ref, sin_ref, norm_mix_ref, w_in_ref, hist_out_ref, state_out_ref):
    u, _, qh, kh, vh, _ = _mixer_inputs(meta_ref[...], norm_mix_ref[...], w_in_ref,
                                        cos_ref[...], sin_ref[...])
    hist_out_ref[...] = u[N_META - 2:N_META, :]
    tabs = _decay_tables(N_META)
    for h in range(RET_HEADS):
        _, _, k_decay, _ = tabs[h]
        kd = (kh[h] * k_decay).astype(BF16)
        state_out_ref[h] = lax.dot_general(kd, vh[h], (((0,), (0,)), ((), ())),
                                           preferred_element_type=F32)


def _prompt_step(x_ref, cos_ref, sin_ref, norm_mix_ref, w_in_ref, conv_w_ref, gn_w_ref, w_out_ref,
                 norm_ffn_ref, w_up_ref, w_down_ref, norm_final_ref,
                 y_ref, hist_out_ref, state_out_ref, ubuf_ref, state_ref, h1_ref,
                 inner_ref, cross_ref, kdec_ref, *, keep_outputs):
    n_chunks = PROMPT_TILE // PROMPT_CHUNK
    n_blocks = D_FF // FF_BLOCK
    assert n_chunks * RET_HEADS == 2 * n_blocks
    half = FF_BLOCK // 2

    x = x_ref[0]
    u, b_gate, qh, kh, vh, g = _mixer_inputs(x, norm_mix_ref[...], w_in_ref,
                                             cos_ref[...], sin_ref[...])
    conv_out, new_hist = _conv_group(u, b_gate, conv_w_ref[...], ubuf_ref, 1, PROMPT_TILE)
    tabs = [(inner_ref[h], cross_ref[h], kdec_ref[h], math.exp(PROMPT_CHUNK * LOG_GAMMA[h]))
            for h in range(RET_HEADS)]
    st = [state_ref[h] for h in range(RET_HEADS)]
    o_chunks = [[None] * n_chunks for _ in range(RET_HEADS)]
    h1_prev = h1_ref[...]
    n2 = _rmsnorm(h1_prev, norm_ffn_ref[...]).astype(BF16)
    carry = {"f": None, "hid": None, "hid_lo": None}

    def up_lo():
        up = jnp.maximum(_dot(n2, w_up_ref[:, :half]), 0.0)
        carry["hid_lo"] = (up * up).astype(BF16)

    def up_hi():
        up = jnp.maximum(_dot(n2, w_up_ref[:, half:FF_BLOCK]), 0.0)
        carry["hid"] = jnp.concatenate([carry["hid_lo"], (up * up).astype(BF16)], axis=1)

    def up(j):
        carry["hid"] = _ffn_up(n2, w_up_ref, j)

    def down(j):
        d = _ffn_down(carry["hid"], w_down_ref, j)
        carry["f"] = d if carry["f"] is None else carry["f"] + d

    items = [up_lo, up_hi, functools.partial(down, 0)]
    for j in range(1, n_blocks):
        items += [functools.partial(up, j), functools.partial(down, j)]
    assert len(items) == n_chunks * RET_HEADS + 1
    items.reverse()

    items.pop()()
    for c in range(n_chunks):
        rows = slice(c * PROMPT_CHUNK, (c + 1) * PROMPT_CHUNK)
        for h in range(RET_HEADS):
            o_chunks[h][c], st[h] = _retention_chunk(qh[h][rows], kh[h][rows], vh[h][rows],
                                                     st[h], tabs[h])
            items.pop()()
    y_ref[0] = _rmsnorm(h1_prev + carry["f"], norm_final_ref[...])

    o_heads = [jnp.concatenate(o_chunks[h], axis=0) for h in range(RET_HEADS)]
    ret_out = _group_norm_gate(o_heads, g, gn_w_ref[...])
    h1_ref[...] = x + (_dot(conv_out, w_out_ref[:D_CONV, :]) + _dot(ret_out, w_out_ref[D_CONV:, :]))
    ubuf_ref[:, SUBLANES - 2:SUBLANES, :] = new_hist
    for h in range(RET_HEADS):
        state_ref[h] = st[h]

    @pl.when(keep_outputs)
    def _():
        hist_out_ref[...] = new_hist
        for h in range(RET_HEADS):
            state_out_ref[0, h] = st[h]


def _prompt_kernel(x_ref, cos_ref, sin_ref, hist0_ref, state0_ref, *refs, tiles_per_stream, n_tiles):
    t = pl.program_id(0)
    ubuf_ref, state_ref, h1_ref, inner_ref, cross_ref, kdec_ref = refs[-6:]

    @pl.when(t == 0)
    def _():
        h1_ref[...] = jnp.zeros_like(h1_ref)
        for h, (inner, cross, kdec, _) in enumerate(_decay_tables(PROMPT_CHUNK)):
            inner_ref[h] = inner
            cross_ref[h] = cross
            kdec_ref[h] = kdec

    @pl.when(t % tiles_per_stream == 0)
    def _():
        ubuf_ref[0, SUBLANES - 2:SUBLANES, :] = hist0_ref[...]
        state_ref[...] = state0_ref[...]

    _prompt_step(x_ref, cos_ref, sin_ref, *refs, keep_outputs=t < n_tiles)


def _sample_kernel(x_ref, cos_ref, sin_ref, hist_ref, state_in_ref,
                   norm_mix_ref, w_in_ref, conv_w_ref, gn_w_ref, w_out_ref, norm_ffn_ref,
                   w_up_ref, w_down_ref, norm_final_ref,
                   y_ref, hist_out_ref, state_out_ref, ubuf_ref):
    n, seg = SAMPLE_STREAMS, x_ref.shape[1]
    states = [[state_in_ref[i, h] for h in range(RET_HEADS)] for i in range(n)]
    cos_t = jnp.concatenate([cos_ref[...]] * n, axis=0)
    sin_t = jnp.concatenate([sin_ref[...]] * n, axis=0)
    y, new_hist, new_states = _layer(
        x_ref[...].reshape(n * seg, D_MODEL), hist_ref[...], states, n, seg, seg, ubuf_ref,
        cos_t, sin_t, norm_mix_ref[...], w_in_ref, conv_w_ref[...],
        gn_w_ref[...], w_out_ref, norm_ffn_ref[...], w_up_ref, w_down_ref, norm_final_ref[...])
    y_ref[...] = y.reshape(n, seg, D_MODEL)
    hist_out_ref[...] = new_hist
    for i in range(n):
        for h in range(RET_HEADS):
            state_out_ref[i, h] = new_states[i][h]


def _rope_tables(first_pos, n):
    half = RET_HEAD_DIM // 2
    inv_freq = ROPE_BASE ** (-np.arange(half, dtype=np.float64) / half)
    ang = (first_pos + np.arange(n, dtype=np.float64))[:, None] * inv_freq[None, :]
    cos, sin = np.cos(ang), np.sin(ang)
    return (jnp.asarray(np.concatenate([cos, cos], axis=-1), dtype=F32),
            jnp.asarray(np.concatenate([-sin, sin], axis=-1), dtype=F32))


def _const_spec(shape):
    nd = len(shape)
    return pl.BlockSpec(shape, lambda *_: (0,) * nd, pipeline_mode=pl.Buffered(1))


def kernel(x_prompt, x_sample, state_conv, state_ret, meta_tokens, norm_mix_w, w_in, conv_w,
           ret_gn_w, w_out, norm_ffn_w, w_up, w_down, norm_final_w):
    depth = w_in.shape[0]
    assert depth == 1, "single-layer step"
    batch, seq, _ = x_prompt.shape
    dec_batch, dec_seq, _ = x_sample.shape
    assert seq % PROMPT_TILE == 0 and dec_batch % SAMPLE_STREAMS == 0

    w_in_b, w_out_b = w_in[0].astype(BF16), w_out[0].astype(BF16)
    w_up_b, w_down_b = w_up[0].astype(BF16), w_down[0].astype(BF16)
    norm_mix, norm_ffn = norm_mix_w[0][None], norm_ffn_w[0][None]
    norm_final = norm_final_w[None]
    gn_w, cw = ret_gn_w[0][None], conv_w[0]

    cos_m, sin_m = _rope_tables(-N_META, N_META)
    cos_p, sin_p = _rope_tables(0, seq)
    cos_s, sin_s = _rope_tables(PAST_LEN, dec_seq)

    params = (norm_mix, w_in_b, cw, gn_w, w_out_b, norm_ffn, w_up_b, w_down_b, norm_final)
    param_specs = [_const_spec(p.shape) for p in params]
    d_in = w_in_b.shape[1]

    meta_hist, meta_state = pl.pallas_call(
        _meta_kernel,
        out_shape=(jax.ShapeDtypeStruct((CONV_WIDTH - 1, D_CONV), F32),
                   jax.ShapeDtypeStruct((RET_HEADS, RET_HEAD_DIM, RET_HEAD_DIM), F32)),
        compiler_params=pltpu.CompilerParams(vmem_limit_bytes=VMEM_LIMIT_BYTES),
        name="meta_state",
    )(meta_tokens, cos_m, sin_m, norm_mix, w_in_b)

    tiles_per_stream = seq // PROMPT_TILE
    n_tiles = batch * tiles_per_stream

    def cur(t):
        return jnp.minimum(t, n_tiles - 1)

    def prev(t):
        return jnp.maximum(t - 1, 0)

    y_prompt, conv_p, ret_p = pl.pallas_call(
        functools.partial(_prompt_kernel, tiles_per_stream=tiles_per_stream, n_tiles=n_tiles),
        grid=(n_tiles + 1,),
        in_specs=[
            pl.BlockSpec((1, PROMPT_TILE, D_MODEL),
                         lambda t: (cur(t) // tiles_per_stream, cur(t) % tiles_per_stream, 0)),
            pl.BlockSpec((PROMPT_TILE, RET_HEAD_DIM), lambda t: (cur(t) % tiles_per_stream, 0)),
            pl.BlockSpec((PROMPT_TILE, RET_HEAD_DIM), lambda t: (cur(t) % tiles_per_stream, 0)),
            _const_spec(meta_hist.shape),
            _const_spec(meta_state.shape),
        ] + param_specs,
        out_specs=(
            pl.BlockSpec((1, PROMPT_TILE, D_MODEL),
                         lambda t: (prev(t) // tiles_per_stream, prev(t) % tiles_per_stream, 0)),
            pl.BlockSpec((1, CONV_WIDTH - 1, D_CONV), lambda t: (cur(t) // tiles_per_stream, 0, 0)),
            pl.BlockSpec((1, RET_HEADS, RET_HEAD_DIM, RET_HEAD_DIM),
                         lambda t: (cur(t) // tiles_per_stream, 0, 0, 0)),
        ),
        out_shape=(
            jax.ShapeDtypeStruct((batch, seq, D_MODEL), F32),
            jax.ShapeDtypeStruct((batch, CONV_WIDTH - 1, D_CONV), F32),
            jax.ShapeDtypeStruct((batch, RET_HEADS, RET_HEAD_DIM, RET_HEAD_DIM), F32),
        ),
        scratch_shapes=[
            pltpu.VMEM((1, PROMPT_TILE + SUBLANES, D_CONV), F32),
            pltpu.VMEM((RET_HEADS, RET_HEAD_DIM, RET_HEAD_DIM), F32),
            pltpu.VMEM((PROMPT_TILE, D_MODEL), F32),
            pltpu.VMEM((RET_HEADS, PROMPT_CHUNK, PROMPT_CHUNK), F32),
            pltpu.VMEM((RET_HEADS, PROMPT_CHUNK, RET_HEAD_DIM), F32),
            pltpu.VMEM((RET_HEADS, PROMPT_CHUNK, RET_HEAD_DIM), F32),
        ],
        compiler_params=pltpu.CompilerParams(
            dimension_semantics=("arbitrary",),
            vmem_limit_bytes=VMEM_LIMIT_BYTES),
        name="prompt_step",
    )(x_prompt, cos_p, sin_p, meta_hist, meta_state, *params)

    ns = SAMPLE_STREAMS
    y_sample, conv_s, ret_s = pl.pallas_call(
        _sample_kernel,
        grid=(dec_batch // ns,),
        in_specs=[
            pl.BlockSpec((ns, dec_seq, D_MODEL), lambda b: (b, 0, 0)),
            _const_spec(cos_s.shape),
            _const_spec(sin_s.shape),
            pl.BlockSpec((ns, CONV_WIDTH - 1, D_CONV), lambda b: (b, 0, 0)),
            pl.BlockSpec((ns, RET_HEADS, RET_HEAD_DIM, RET_HEAD_DIM), lambda b: (b, 0, 0, 0)),
        ] + param_specs,
        out_specs=(
            pl.BlockSpec((ns, dec_seq, D_MODEL), lambda b: (b, 0, 0)),
            pl.BlockSpec((ns, CONV_WIDTH - 1, D_CONV), lambda b: (b, 0, 0)),
            pl.BlockSpec((ns, RET_HEADS, RET_HEAD_DIM, RET_HEAD_DIM), lambda b: (b, 0, 0, 0)),
        ),
        out_shape=(
            jax.ShapeDtypeStruct((dec_batch, dec_seq, D_MODEL), F32),
            jax.ShapeDtypeStruct((dec_batch, CONV_WIDTH - 1, D_CONV), F32),
            jax.ShapeDtypeStruct((dec_batch, RET_HEADS, RET_HEAD_DIM, RET_HEAD_DIM), F32),
        ),
        scratch_shapes=[pltpu.VMEM((ns, dec_seq + SUBLANES, D_CONV), F32)],
        compiler_params=pltpu.CompilerParams(
            dimension_semantics=("arbitrary",),
            vmem_limit_bytes=VMEM_LIMIT_BYTES),
        name="sample_step",
    )(x_sample, cos_s, sin_s, state_conv[0], state_ret[0], *params)

    return (y_prompt, y_sample, conv_p[None], ret_p[None], conv_s[None], ret_s[None])
```

```python
import functools
import math

import jax
import jax.numpy as jnp
import numpy as np
from jax import lax
from jax.experimental import pallas as pl
from jax.experimental.pallas import tpu as pltpu

D_MODEL = 1024
N_META = 16
PAST_LEN = 2048
D_CONV = 512
CONV_WIDTH = 3
RET_HEADS = 4
RET_HEAD_DIM = 128
D_RET = RET_HEADS * RET_HEAD_DIM
D_FF = 4 * D_MODEL
ROPE_BASE = 10000.0
NORM_EPS = 1e-6
GN_EPS = 1e-5
K_SCALE = RET_HEAD_DIM ** -0.5
LOG_GAMMA = tuple(math.log(1.0 - 2.0 ** (-5.0 - h)) for h in range(RET_HEADS))

SUBLANES = 8
VMEM_LIMIT_BYTES = 58 * 1024 * 1024

PROMPT_TILE = 512
PROMPT_CHUNK = 256
SAMPLE_STREAMS = 8
FF_BLOCK = 1024

BF16 = jnp.bfloat16
F32 = jnp.float32


def _dot(a, b):
    return jnp.dot(a, b, preferred_element_type=F32)


def _rmsnorm(x, w):
    ms = jnp.mean(x * x, axis=-1, keepdims=True)
    return x * lax.rsqrt(ms + NORM_EPS) * w


def _rotate(x, cos_t, sin_t):
    return x * cos_t + pltpu.roll(x, RET_HEAD_DIM // 2, 1) * sin_t


def _decay_tables(chunk):
    row = lax.broadcasted_iota(jnp.int32, (chunk, chunk), 0)
    col = lax.broadcasted_iota(jnp.int32, (chunk, chunk), 1)
    diff = (row - col).astype(F32)
    rowv = lax.broadcasted_iota(jnp.int32, (chunk, RET_HEAD_DIM), 0).astype(F32)
    tabs = []
    for lg in LOG_GAMMA:
        inner = jnp.where(diff >= 0.0, jnp.exp(jnp.maximum(diff, 0.0) * lg), 0.0) * K_SCALE
        cross = jnp.exp((rowv + 1.0) * lg)
        kdec = jnp.exp((chunk - 1.0 - rowv) * lg) * K_SCALE
        tabs.append((inner, cross, kdec, math.exp(chunk * lg)))
    return tabs


def _mixer_inputs(x, norm_w, w_in_ref, cos_t, sin_t):
    hn = _rmsnorm(x, norm_w).astype(BF16)
    pc = _dot(hn, w_in_ref[:, : 3 * D_CONV])
    xt = pc[:, :D_CONV]
    b_gate = pc[:, D_CONV: 2 * D_CONV]
    c_gate = pc[:, 2 * D_CONV:]
    u = c_gate * xt
    pr = _dot(hn, w_in_ref[:, 3 * D_CONV:])
    q, k, v, g = (pr[:, i * D_RET:(i + 1) * D_RET] for i in range(4))
    qh, kh, vh = [], [], []
    for h in range(RET_HEADS):
        sl = slice(h * RET_HEAD_DIM, (h + 1) * RET_HEAD_DIM)
        qh.append(_rotate(q[:, sl], cos_t, sin_t).astype(BF16))
        kh.append(_rotate(k[:, sl], cos_t, sin_t))
        vh.append(v[:, sl].astype(BF16))
    return u, b_gate, qh, kh, vh, g


def _retention_scores(qc, kc, vc, s, tabs):
    _, _, k_decay, gamma_l = tabs
    scores = lax.dot_general(qc, kc.astype(BF16), (((1,), (1,)), ((), ())),
                             preferred_element_type=F32)
    kd = (kc * k_decay).astype(BF16)
    s_new = gamma_l * s + lax.dot_general(kd, vc, (((0,), (0,)), ((), ())),
                                          preferred_element_type=F32)
    return scores, s_new


def _retention_output(scores, qc, vc, s, tabs):
    inner_decay, cross_decay, _, _ = tabs
    p = (scores * inner_decay).astype(BF16)
    return _dot(p, vc) + _dot(qc, s.astype(BF16)) * cross_decay


def _retention_chunk(qc, kc, vc, s, tabs):
    scores, s_new = _retention_scores(qc, kc, vc, s, tabs)
    return _retention_output(scores, qc, vc, s, tabs), s_new


def _retention(qh, kh, vh, states, n_streams, seg, chunk, tabs):
    new_states = [list(st) for st in states]
    groups = [(n, c) for n in range(n_streams) for c in range(seg // chunk)]

    def first_stage(n, c):
        lo = n * seg + c * chunk
        return [_retention_scores(qh[h][lo:lo + chunk], kh[h][lo:lo + chunk], vh[h][lo:lo + chunk],
                                  new_states[n][h], tabs[h]) for h in range(RET_HEADS)]

    rows = []
    staged = first_stage(*groups[0])
    for i, (n, c) in enumerate(groups):
        lo = n * seg + c * chunk
        old = list(new_states[n])
        for h in range(RET_HEADS):
            new_states[n][h] = staged[h][1]
        this = staged
        if i + 1 < len(groups):
            staged = first_stage(*groups[i + 1])
        rows.append([_retention_output(this[h][0], qh[h][lo:lo + chunk], vh[h][lo:lo + chunk],
                                       old[h], tabs[h]) for h in range(RET_HEADS)])
    o_heads = [jnp.concatenate([r[h] for r in rows], axis=0) if len(rows) > 1 else rows[0][h]
               for h in range(RET_HEADS)]
    return o_heads, new_states


def _group_norm_gate(o_heads, g, gn_w):
    outs = []
    for h in range(RET_HEADS):
        o = o_heads[h]
        mu = jnp.mean(o, axis=-1, keepdims=True)
        d = o - mu
        var = jnp.mean(d * d, axis=-1, keepdims=True)
        sl = slice(h * RET_HEAD_DIM, (h + 1) * RET_HEAD_DIM)
        gh = g[:, sl]
        outs.append((gh * jax.nn.sigmoid(gh)) * (d * lax.rsqrt(var + GN_EPS) * gn_w[:, sl]))
    return jnp.concatenate(outs, axis=1).astype(BF16)


def _ffn_up(n2, w_up_ref, j):
    up = jnp.maximum(_dot(n2, w_up_ref[:, j * FF_BLOCK:(j + 1) * FF_BLOCK]), 0.0)
    return (up * up).astype(BF16)


def _ffn_down(hid, w_down_ref, j):
    return _dot(hid, w_down_ref[j * FF_BLOCK:(j + 1) * FF_BLOCK, :])


def _ffn_and_final(h1, norm_ffn_w, w_up_ref, w_down_ref, norm_final_w):
    n2 = _rmsnorm(h1, norm_ffn_w).astype(BF16)
    f = None
    for j in range(D_FF // FF_BLOCK):
        d = _ffn_down(_ffn_up(n2, w_up_ref, j), w_down_ref, j)
        f = d if f is None else f + d
    return _rmsnorm(h1 + f, norm_final_w)


def _conv_group(u, b_gate, conv_w, ubuf_ref, n_streams, seg):
    ubuf_ref[:, SUBLANES:SUBLANES + seg, :] = u.reshape(n_streams, seg, D_CONV)
    u1 = ubuf_ref[:, SUBLANES - 1:SUBLANES - 1 + seg, :].reshape(n_streams * seg, D_CONV)
    u2 = ubuf_ref[:, SUBLANES - 2:SUBLANES - 2 + seg, :].reshape(n_streams * seg, D_CONV)
    conv_y = conv_w[0:1, :] * u2 + conv_w[1:2, :] * u1 + conv_w[2:3, :] * u
    new_hist = ubuf_ref[:, SUBLANES + seg - 2:SUBLANES + seg, :]
    return (b_gate * conv_y).astype(BF16), new_hist


def _layer(x, hist_ref_init, states, n_streams, seg, chunk, ubuf_ref, cos_t, sin_t,
           norm_mix_w, w_in_ref, conv_w, gn_w, w_out_ref, norm_ffn_w, w_up_ref, w_down_ref,
           norm_final_w):
    u, b_gate, qh, kh, vh, g = _mixer_inputs(x, norm_mix_w, w_in_ref, cos_t, sin_t)
    if hist_ref_init is not None:
        ubuf_ref[:, SUBLANES - 2:SUBLANES, :] = hist_ref_init
    conv_out, new_hist = _conv_group(u, b_gate, conv_w, ubuf_ref, n_streams, seg)
    tabs = _decay_tables(chunk)
    o_heads, new_states = _retention(qh, kh, vh, states, n_streams, seg, chunk, tabs)
    ret_out = _group_norm_gate(o_heads, g, gn_w)
    h1 = x + (_dot(conv_out, w_out_ref[:D_CONV, :]) + _dot(ret_out, w_out_ref[D_CONV:, :]))
    y = _ffn_and_final(h1, norm_ffn_w, w_up_ref, w_down_ref, norm_final_w)
    return y, new_hist, new_states


def _meta_kernel(meta_ref, cos_ref, sin_ref, norm_mix_ref, w_in_ref, hist_out_ref, state_out_ref):
    u, _, qh, kh, vh, _ = _mixer_inputs(meta_ref[...], norm_mix_ref[...], w_in_ref,
                                        cos_ref[...], sin_ref[...])
    hist_out_ref[...] = u[N_META - 2:N_META, :]
    tabs = _decay_tables(N_META)
    for h in range(RET_HEADS):
        _, _, k_decay, _ = tabs[h]
        kd = (kh[h] * k_decay).astype(BF16)
        state_out_ref[h] = lax.dot_general(kd, vh[h], (((0,), (0,)), ((), ())),
                                           preferred_element_type=F32)


def _prompt_step(x_ref, cos_ref, sin_ref, norm_mix_ref, w_in_ref, conv_w_ref, gn_w_ref, w_out_ref,
                 norm_ffn_ref, w_up_ref, w_down_ref, norm_final_ref,
                 y_ref, hist_out_ref, state_out_ref, ubuf_ref, state_ref, h1_ref,
                 inner_ref, cross_ref, kdec_ref, *, keep_outputs):
    n_chunks = PROMPT_TILE // PROMPT_CHUNK
    n_blocks = D_FF // FF_BLOCK
    assert n_chunks * RET_HEADS == 2 * n_blocks
    half = FF_BLOCK // 2

    x = x_ref[0]
    u, b_gate, qh, kh, vh, g = _mixer_inputs(x, norm_mix_ref[...], w_in_ref,
                                             cos_ref[...], sin_ref[...])
    conv_out, new_hist = _conv_group(u, b_gate, conv_w_ref[...], ubuf_ref, 1, PROMPT_TILE)
    tabs = [(inner_ref[h], cross_ref[h], kdec_ref[h], math.exp(PROMPT_CHUNK * LOG_GAMMA[h]))
            for h in range(RET_HEADS)]
    st = [state_ref[h] for h in range(RET_HEADS)]
    o_chunks = [[None] * n_chunks for _ in range(RET_HEADS)]
    h1_prev = h1_ref[...]
    n2 = _rmsnorm(h1_prev, norm_ffn_ref[...]).astype(BF16)
    carry = {"f": None, "hid": None, "hid_lo": None}

    def up_lo():
        up = jnp.maximum(_dot(n2, w_up_ref[:, :half]), 0.0)
        carry["hid_lo"] = (up * up).astype(BF16)

    def up_hi():
        up = jnp.maximum(_dot(n2, w_up_ref[:, half:FF_BLOCK]), 0.0)
        carry["hid"] = jnp.concatenate([carry["hid_lo"], (up * up).astype(BF16)], axis=1)

    def up(j):
        carry["hid"] = _ffn_up(n2, w_up_ref, j)

    def down(j):
        d = _ffn_down(carry["hid"], w_down_ref, j)
        carry["f"] = d if carry["f"] is None else carry["f"] + d

    items = [up_lo, up_hi, functools.partial(down, 0)]
    for j in range(1, n_blocks):
        items += [functools.partial(up, j), functools.partial(down, j)]
    assert len(items) == n_chunks * RET_HEADS + 1
    items.reverse()

    pieces = [(c, h) for c in range(n_chunks) for h in range(RET_HEADS)]

    def first_stage(c, h):
        rows = slice(c * PROMPT_CHUNK, (c + 1) * PROMPT_CHUNK)
        return _retention_scores(qh[h][rows], kh[h][rows], vh[h][rows], st[h], tabs[h])

    items.pop()()
    scores, s_new = first_stage(*pieces[0])
    for i, (c, h) in enumerate(pieces):
        rows = slice(c * PROMPT_CHUNK, (c + 1) * PROMPT_CHUNK)
        items.pop()()
        o_chunks[h][c] = _retention_output(scores, qh[h][rows], vh[h][rows], st[h], tabs[h])
        st[h] = s_new
        if i + 1 < len(pieces):
            scores, s_new = first_stage(*pieces[i + 1])
    y_ref[0] = _rmsnorm(h1_prev + carry["f"], norm_final_ref[...])

    o_heads = [jnp.concatenate(o_chunks[h], axis=0) for h in range(RET_HEADS)]
    ret_out = _group_norm_gate(o_heads, g, gn_w_ref[...])
    h1_ref[...] = x + (_dot(conv_out, w_out_ref[:D_CONV, :]) + _dot(ret_out, w_out_ref[D_CONV:, :]))
    ubuf_ref[:, SUBLANES - 2:SUBLANES, :] = new_hist
    for h in range(RET_HEADS):
        state_ref[h] = st[h]

    @pl.when(keep_outputs)
    def _():
        hist_out_ref[...] = new_hist
        for h in range(RET_HEADS):
            state_out_ref[0, h] = st[h]


def _prompt_kernel(x_ref, cos_ref, sin_ref, hist0_ref, state0_ref, *refs, tiles_per_stream, n_tiles):
    t = pl.program_id(0)
    ubuf_ref, state_ref, h1_ref, inner_ref, cross_ref, kdec_ref = refs[-6:]

    @pl.when(t == 0)
    def _():
        h1_ref[...] = jnp.zeros_like(h1_ref)
        for h, (inner, cross, kdec, _) in enumerate(_decay_tables(PROMPT_CHUNK)):
            inner_ref[h] = inner
            cross_ref[h] = cross
            kdec_ref[h] = kdec

    @pl.when(t % tiles_per_stream == 0)
    def _():
        ubuf_ref[0, SUBLANES - 2:SUBLANES, :] = hist0_ref[...]
        state_ref[...] = state0_ref[...]

    _prompt_step(x_ref, cos_ref, sin_ref, *refs, keep_outputs=t < n_tiles)


def _sample_kernel(x_ref, cos_ref, sin_ref, hist_ref, state_in_ref,
                   norm_mix_ref, w_in_ref, conv_w_ref, gn_w_ref, w_out_ref, norm_ffn_ref,
                   w_up_ref, w_down_ref, norm_final_ref,
                   y_ref, hist_out_ref, state_out_ref, ubuf_ref):
    n, seg = SAMPLE_STREAMS, x_ref.shape[1]
    states = [[state_in_ref[i, h] for h in range(RET_HEADS)] for i in range(n)]
    cos_t = jnp.concatenate([cos_ref[...]] * n, axis=0)
    sin_t = jnp.concatenate([sin_ref[...]] * n, axis=0)
    y, new_hist, new_states = _layer(
        x_ref[...].reshape(n * seg, D_MODEL), hist_ref[...], states, n, seg, seg, ubuf_ref,
        cos_t, sin_t, norm_mix_ref[...], w_in_ref, conv_w_ref[...],
        gn_w_ref[...], w_out_ref, norm_ffn_ref[...], w_up_ref, w_down_ref, norm_final_ref[...])
    y_ref[...] = y.reshape(n, seg, D_MODEL)
    hist_out_ref[...] = new_hist
    for i in range(n):
        for h in range(RET_HEADS):
            state_out_ref[i, h] = new_states[i][h]


def _rope_tables(first_pos, n):
    half = RET_HEAD_DIM // 2
    inv_freq = ROPE_BASE ** (-np.arange(half, dtype=np.float64) / half)
    ang = (first_pos + np.arange(n, dtype=np.float64))[:, None] * inv_freq[None, :]
    cos, sin = np.cos(ang), np.sin(ang)
    return (jnp.asarray(np.concatenate([cos, cos], axis=-1), dtype=F32),
            jnp.asarray(np.concatenate([-sin, sin], axis=-1), dtype=F32))


def _const_spec(shape):
    nd = len(shape)
    return pl.BlockSpec(shape, lambda *_: (0,) * nd, pipeline_mode=pl.Buffered(1))


def kernel(x_prompt, x_sample, state_conv, state_ret, meta_tokens, norm_mix_w, w_in, conv_w,
           ret_gn_w, w_out, norm_ffn_w, w_up, w_down, norm_final_w):
    depth = w_in.shape[0]
    assert depth == 1, "single-layer step"
    batch, seq, _ = x_prompt.shape
    dec_batch, dec_seq, _ = x_sample.shape
    assert seq % PROMPT_TILE == 0 and dec_batch % SAMPLE_STREAMS == 0

    w_in_b, w_out_b = w_in[0].astype(BF16), w_out[0].astype(BF16)
    w_up_b, w_down_b = w_up[0].astype(BF16), w_down[0].astype(BF16)
    norm_mix, norm_ffn = norm_mix_w[0][None], norm_ffn_w[0][None]
    norm_final = norm_final_w[None]
    gn_w, cw = ret_gn_w[0][None], conv_w[0]

    cos_m, sin_m = _rope_tables(-N_META, N_META)
    cos_p, sin_p = _rope_tables(0, seq)
    cos_s, sin_s = _rope_tables(PAST_LEN, dec_seq)

    params = (norm_mix, w_in_b, cw, gn_w, w_out_b, norm_ffn, w_up_b, w_down_b, norm_final)
    param_specs = [_const_spec(p.shape) for p in params]
    d_in = w_in_b.shape[1]

    meta_hist, meta_state = pl.pallas_call(
        _meta_kernel,
        out_shape=(jax.ShapeDtypeStruct((CONV_WIDTH - 1, D_CONV), F32),
                   jax.ShapeDtypeStruct((RET_HEADS, RET_HEAD_DIM, RET_HEAD_DIM), F32)),
        compiler_params=pltpu.CompilerParams(vmem_limit_bytes=VMEM_LIMIT_BYTES),
        name="meta_state",
    )(meta_tokens, cos_m, sin_m, norm_mix, w_in_b)

    tiles_per_stream = seq // PROMPT_TILE
    n_tiles = batch * tiles_per_stream

    def cur(t):
        return jnp.minimum(t, n_tiles - 1)

    def prev(t):
        return jnp.maximum(t - 1, 0)

    y_prompt, conv_p, ret_p = pl.pallas_call(
        functools.partial(_prompt_kernel, tiles_per_stream=tiles_per_stream, n_tiles=n_tiles),
        grid=(n_tiles + 1,),
        in_specs=[
            pl.BlockSpec((1, PROMPT_TILE, D_MODEL),
                         lambda t: (cur(t) // tiles_per_stream, cur(t) % tiles_per_stream, 0)),
            pl.BlockSpec((PROMPT_TILE, RET_HEAD_DIM), lambda t: (cur(t) % tiles_per_stream, 0)),
            pl.BlockSpec((PROMPT_TILE, RET_HEAD_DIM), lambda t: (cur(t) % tiles_per_stream, 0)),
            _const_spec(meta_hist.shape),
            _const_spec(meta_state.shape),
        ] + param_specs,
        out_specs=(
            pl.BlockSpec((1, PROMPT_TILE, D_MODEL),
                         lambda t: (prev(t) // tiles_per_stream, prev(t) % tiles_per_stream, 0)),
            pl.BlockSpec((1, CONV_WIDTH - 1, D_CONV), lambda t: (cur(t) // tiles_per_stream, 0, 0)),
            pl.BlockSpec((1, RET_HEADS, RET_HEAD_DIM, RET_HEAD_DIM),
                         lambda t: (cur(t) // tiles_per_stream, 0, 0, 0)),
        ),
        out_shape=(
            jax.ShapeDtypeStruct((batch, seq, D_MODEL), F32),
            jax.ShapeDtypeStruct((batch, CONV_WIDTH - 1, D_CONV), F32),
            jax.ShapeDtypeStruct((batch, RET_HEADS, RET_HEAD_DIM, RET_HEAD_DIM), F32),
        ),
        scratch_shapes=[
            pltpu.VMEM((1, PROMPT_TILE + SUBLANES, D_CONV), F32),
            pltpu.VMEM((RET_HEADS, RET_HEAD_DIM, RET_HEAD_DIM), F32),
            pltpu.VMEM((PROMPT_TILE, D_MODEL), F32),
            pltpu.VMEM((RET_HEADS, PROMPT_CHUNK, PROMPT_CHUNK), F32),
            pltpu.VMEM((RET_HEADS, PROMPT_CHUNK, RET_HEAD_DIM), F32),
            pltpu.VMEM((RET_HEADS, PROMPT_CHUNK, RET_HEAD_DIM), F32),
        ],
        compiler_params=pltpu.CompilerParams(
            dimension_semantics=("arbitrary",),
            vmem_limit_bytes=VMEM_LIMIT_BYTES),
        name="prompt_step",
    )(x_prompt, cos_p, sin_p, meta_hist, meta_state, *params)

    ns = SAMPLE_STREAMS
    y_sample, conv_s, ret_s = pl.pallas_call(
        _sample_kernel,
        grid=(dec_batch // ns,),
        in_specs=[
            pl.BlockSpec((ns, dec_seq, D_MODEL), lambda b: (b, 0, 0)),
            _const_spec(cos_s.shape),
            _const_spec(sin_s.shape),
            pl.BlockSpec((ns, CONV_WIDTH - 1, D_CONV), lambda b: (b, 0, 0)),
            pl.BlockSpec((ns, RET_HEADS, RET_HEAD_DIM, RET_HEAD_DIM), lambda b: (b, 0, 0, 0)),
        ] + param_specs,
        out_specs=(
            pl.BlockSpec((ns, dec_seq, D_MODEL), lambda b: (b, 0, 0)),
            pl.BlockSpec((ns, CONV_WIDTH - 1, D_CONV), lambda b: (b, 0, 0)),
            pl.BlockSpec((ns, RET_HEADS, RET_HEAD_DIM, RET_HEAD_DIM), lambda b: (b, 0, 0, 0)),
        ),
        out_shape=(
            jax.ShapeDtypeStruct((dec_batch, dec_seq, D_MODEL), F32),
            jax.ShapeDtypeStruct((dec_batch, CONV_WIDTH - 1, D_CONV), F32),
            jax.ShapeDtypeStruct((dec_batch, RET_HEADS, RET_HEAD_DIM, RET_HEAD_DIM), F32),
        ),
        scratch_shapes=[pltpu.VMEM((ns, dec_seq + SUBLANES, D_CONV), F32)],
        compiler_params=pltpu.CompilerParams(
            dimension_semantics=("arbitrary",),
            vmem_limit_bytes=VMEM_LIMIT_BYTES),
        name="sample_step",
    )(x_sample, cos_s, sin_s, state_conv[0], state_ret[0], *params)

    return (y_prompt, y_sample, conv_p[None], ret_p[None], conv_s[None], ret_s[None])
```

```python
import functools
import math

import jax
import jax.numpy as jnp
import numpy as np
from jax import lax
from jax.experimental import pallas as pl
from jax.experimental.pallas import tpu as pltpu

D_MODEL = 1024
N_META = 16
PAST_LEN = 2048
D_CONV = 512
CONV_WIDTH = 3
RET_HEADS = 4
RET_HEAD_DIM = 128
D_RET = RET_HEADS * RET_HEAD_DIM
D_FF = 4 * D_MODEL
ROPE_BASE = 10000.0
NORM_EPS = 1e-6
GN_EPS = 1e-5
K_SCALE = RET_HEAD_DIM ** -0.5
LOG_GAMMA = tuple(math.log(1.0 - 2.0 ** (-5.0 - h)) for h in range(RET_HEADS))

SUBLANES = 8
VMEM_LIMIT_BYTES = 58 * 1024 * 1024

PROMPT_TILE = 512
PROMPT_CHUNK = 256
SAMPLE_STREAMS = 8
FF_BLOCK = 1024

BF16 = jnp.bfloat16
F32 = jnp.float32


def _dot(a, b):
    return jnp.dot(a, b, preferred_element_type=F32)


def _rmsnorm(x, w):
    ms = jnp.mean(x * x, axis=-1, keepdims=True)
    return x * lax.rsqrt(ms + NORM_EPS) * w


def _rotate(x, cos_t, sin_t):
    return x * cos_t + pltpu.roll(x, RET_HEAD_DIM // 2, 1) * sin_t


def _decay_tables(chunk):
    row = lax.broadcasted_iota(jnp.int32, (chunk, chunk), 0)
    col = lax.broadcasted_iota(jnp.int32, (chunk, chunk), 1)
    diff = (row - col).astype(F32)
    rowv = lax.broadcasted_iota(jnp.int32, (chunk, RET_HEAD_DIM), 0).astype(F32)
    tabs = []
    for lg in LOG_GAMMA:
        inner = jnp.where(diff >= 0.0, jnp.exp(jnp.maximum(diff, 0.0) * lg), 0.0) * K_SCALE
        cross = jnp.exp((rowv + 1.0) * lg)
        kdec = jnp.exp((chunk - 1.0 - rowv) * lg) * K_SCALE
        tabs.append((inner, cross, kdec, math.exp(chunk * lg)))
    return tabs


def _mixer_inputs(x, norm_w, w_in_ref, cos_t, sin_t):
    hn = _rmsnorm(x, norm_w).astype(BF16)
    pc = _dot(hn, w_in_ref[:, : 3 * D_CONV])
    xt = pc[:, :D_CONV]
    b_gate = pc[:, D_CONV: 2 * D_CONV]
    c_gate = pc[:, 2 * D_CONV:]
    u = c_gate * xt
    pr = _dot(hn, w_in_ref[:, 3 * D_CONV:])
    q, k, v, g = (pr[:, i * D_RET:(i + 1) * D_RET] for i in range(4))
    qh, kh, vh = [], [], []
    for h in range(RET_HEADS):
        sl = slice(h * RET_HEAD_DIM, (h + 1) * RET_HEAD_DIM)
        qh.append(_rotate(q[:, sl], cos_t, sin_t).astype(BF16))
        kh.append(_rotate(k[:, sl], cos_t, sin_t))
        vh.append(v[:, sl].astype(BF16))
    return u, b_gate, qh, kh, vh, g


def _retention_scores(qc, kc, vc, s, tabs):
    _, cross_decay, k_decay, gamma_l = tabs
    scores = lax.dot_general(qc, kc.astype(BF16), (((1,), (1,)), ((), ())),
                             preferred_element_type=F32)
    cross = _dot(qc, s.astype(BF16)) * cross_decay
    kd = (kc * k_decay).astype(BF16)
    s_new = gamma_l * s + lax.dot_general(kd, vc, (((0,), (0,)), ((), ())),
                                          preferred_element_type=F32)
    return scores, cross, s_new


def _retention_output(scores, cross, vc, tabs):
    p = (scores * tabs[0]).astype(BF16)
    return _dot(p, vc) + cross


def _retention_chunk(qc, kc, vc, s, tabs):
    scores, cross, s_new = _retention_scores(qc, kc, vc, s, tabs)
    return _retention_output(scores, cross, vc, tabs), s_new


def _retention(qh, kh, vh, states, n_streams, seg, chunk, tabs):
    new_states = [list(st) for st in states]
    groups = [(n, c) for n in range(n_streams) for c in range(seg // chunk)]

    def first_stage(n, c):
        lo = n * seg + c * chunk
        return [_retention_scores(qh[h][lo:lo + chunk], kh[h][lo:lo + chunk], vh[h][lo:lo + chunk],
                                  new_states[n][h], tabs[h]) for h in range(RET_HEADS)]

    rows = []
    staged = first_stage(*groups[0])
    for i, (n, c) in enumerate(groups):
        lo = n * seg + c * chunk
        for h in range(RET_HEADS):
            new_states[n][h] = staged[h][2]
        this = staged
        if i + 1 < len(groups):
            staged = first_stage(*groups[i + 1])
        rows.append([_retention_output(this[h][0], this[h][1], vh[h][lo:lo + chunk], tabs[h])
                     for h in range(RET_HEADS)])
    o_heads = [jnp.concatenate([r[h] for r in rows], axis=0) if len(rows) > 1 else rows[0][h]
               for h in range(RET_HEADS)]
    return o_heads, new_states


def _group_norm_gate(o_heads, g, gn_w):
    outs = []
    for h in range(RET_HEADS):
        o = o_heads[h]
        mu = jnp.mean(o, axis=-1, keepdims=True)
        d = o - mu
        var = jnp.mean(d * d, axis=-1, keepdims=True)
        sl = slice(h * RET_HEAD_DIM, (h + 1) * RET_HEAD_DIM)
        gh = g[:, sl]
        outs.append((gh * jax.nn.sigmoid(gh)) * (d * lax.rsqrt(var + GN_EPS) * gn_w[:, sl]))
    return jnp.concatenate(outs, axis=1).astype(BF16)


def _ffn_up(n2, w_up_ref, j):
    up = jnp.maximum(_dot(n2, w_up_ref[:, j * FF_BLOCK:(j + 1) * FF_BLOCK]), 0.0)
    return (up * up).astype(BF16)


def _ffn_down(hid, w_down_ref, j):
    return _dot(hid, w_down_ref[j * FF_BLOCK:(j + 1) * FF_BLOCK, :])


def _ffn_and_final(h1, norm_ffn_w, w_up_ref, w_down_ref, norm_final_w):
    n2 = _rmsnorm(h1, norm_ffn_w).astype(BF16)
    f = None
    for j in range(D_FF // FF_BLOCK):
        d = _ffn_down(_ffn_up(n2, w_up_ref, j), w_down_ref, j)
        f = d if f is None else f + d
    return _rmsnorm(h1 + f, norm_final_w)


def _conv_group(u, b_gate, conv_w, ubuf_ref, n_streams, seg):
    ubuf_ref[:, SUBLANES:SUBLANES + seg, :] = u.reshape(n_streams, seg, D_CONV)
    u1 = ubuf_ref[:, SUBLANES - 1:SUBLANES - 1 + seg, :].reshape(n_streams * seg, D_CONV)
    u2 = ubuf_ref[:, SUBLANES - 2:SUBLANES - 2 + seg, :].reshape(n_streams * seg, D_CONV)
    conv_y = conv_w[0:1, :] * u2 + conv_w[1:2, :] * u1 + conv_w[2:3, :] * u
    new_hist = ubuf_ref[:, SUBLANES + seg - 2:SUBLANES + seg, :]
    return (b_gate * conv_y).astype(BF16), new_hist


def _layer(x, hist_ref_init, states, n_streams, seg, chunk, ubuf_ref, cos_t, sin_t,
           norm_mix_w, w_in_ref, conv_w, gn_w, w_out_ref, norm_ffn_w, w_up_ref, w_down_ref,
           norm_final_w):
    u, b_gate, qh, kh, vh, g = _mixer_inputs(x, norm_mix_w, w_in_ref, cos_t, sin_t)
    if hist_ref_init is not None:
        ubuf_ref[:, SUBLANES - 2:SUBLANES, :] = hist_ref_init
    conv_out, new_hist = _conv_group(u, b_gate, conv_w, ubuf_ref, n_streams, seg)
    tabs = _decay_tables(chunk)
    o_heads, new_states = _retention(qh, kh, vh, states, n_streams, seg, chunk, tabs)
    ret_out = _group_norm_gate(o_heads, g, gn_w)
    h1 = x + (_dot(conv_out, w_out_ref[:D_CONV, :]) + _dot(ret_out, w_out_ref[D_CONV:, :]))
    y = _ffn_and_final(h1, norm_ffn_w, w_up_ref, w_down_ref, norm_final_w)
    return y, new_hist, new_states


def _meta_kernel(meta_ref, cos_ref, sin_ref, norm_mix_ref, w_in_ref, hist_out_ref, state_out_ref):
    u, _, qh, kh, vh, _ = _mixer_inputs(meta_ref[...], norm_mix_ref[...], w_in_ref,
                                        cos_ref[...], sin_ref[...])
    hist_out_ref[...] = u[N_META - 2:N_META, :]
    tabs = _decay_tables(N_META)
    for h in range(RET_HEADS):
        _, _, k_decay, _ = tabs[h]
        kd = (kh[h] * k_decay).astype(BF16)
        state_out_ref[h] = lax.dot_general(kd, vh[h], (((0,), (0,)), ((), ())),
                                           preferred_element_type=F32)


def _prompt_step(x_ref, cos_ref, sin_ref, norm_mix_ref, w_in_ref, conv_w_ref, gn_w_ref, w_out_ref,
                 norm_ffn_ref, w_up_ref, w_down_ref, norm_final_ref,
                 y_ref, hist_out_ref, state_out_ref, ubuf_ref, state_ref, h1_ref,
                 inner_ref, cross_ref, kdec_ref):
    n_chunks = PROMPT_TILE // PROMPT_CHUNK
    n_blocks = D_FF // FF_BLOCK
    assert n_chunks * RET_HEADS == 2 * n_blocks
    half = FF_BLOCK // 2

    x = x_ref[0]
    u, b_gate, qh, kh, vh, g = _mixer_inputs(x, norm_mix_ref[...], w_in_ref,
                                             cos_ref[...], sin_ref[...])
    conv_out, new_hist = _conv_group(u, b_gate, conv_w_ref[...], ubuf_ref, 1, PROMPT_TILE)
    tabs = [(inner_ref[h], cross_ref[h], kdec_ref[h], math.exp(PROMPT_CHUNK * LOG_GAMMA[h]))
            for h in range(RET_HEADS)]
    st = [state_ref[h] for h in range(RET_HEADS)]
    o_chunks = [[None] * n_chunks for _ in range(RET_HEADS)]
    h1_prev = h1_ref[...]
    n2 = _rmsnorm(h1_prev, norm_ffn_ref[...]).astype(BF16)
    carry = {"f": None, "hid": None, "hid_lo": None}

    def up_lo():
        up = jnp.maximum(_dot(n2, w_up_ref[:, :half]), 0.0)
        carry["hid_lo"] = (up * up).astype(BF16)

    def up_hi():
        up = jnp.maximum(_dot(n2, w_up_ref[:, half:FF_BLOCK]), 0.0)
        carry["hid"] = jnp.concatenate([carry["hid_lo"], (up * up).astype(BF16)], axis=1)

    def up(j):
        carry["hid"] = _ffn_up(n2, w_up_ref, j)

    def down(j):
        d = _ffn_down(carry["hid"], w_down_ref, j)
        carry["f"] = d if carry["f"] is None else carry["f"] + d

    items = [up_lo, up_hi, functools.partial(down, 0)]
    for j in range(1, n_blocks):
        items += [functools.partial(up, j), functools.partial(down, j)]
    assert len(items) == n_chunks * RET_HEADS + 1
    items.reverse()

    items.pop()()
    for c in range(n_chunks):
        rows = slice(c * PROMPT_CHUNK, (c + 1) * PROMPT_CHUNK)
        for h in range(RET_HEADS):
            o_chunks[h][c], st[h] = _retention_chunk(qh[h][rows], kh[h][rows], vh[h][rows],
                                                     st[h], tabs[h])
            items.pop()()
    y_ref[0] = _rmsnorm(h1_prev + carry["f"], norm_final_ref[...])

    o_heads = [jnp.concatenate(o_chunks[h], axis=0) for h in range(RET_HEADS)]
    ret_out = _group_norm_gate(o_heads, g, gn_w_ref[...])
    h1_ref[...] = x + (_dot(conv_out, w_out_ref[:D_CONV, :]) + _dot(ret_out, w_out_ref[D_CONV:, :]))
    ubuf_ref[:, SUBLANES - 2:SUBLANES, :] = new_hist
    hist_out_ref[...] = new_hist
    for h in range(RET_HEADS):
        state_ref[h] = st[h]
        state_out_ref[0, h] = st[h]


def _ffn_drain(norm_ffn_ref, w_up_ref, w_down_ref, norm_final_ref, y_ref, h1_ref):
    h1 = h1_ref[...]
    n2 = _rmsnorm(h1, norm_ffn_ref[...]).astype(BF16)

    def block(j, f):
        lo = pl.multiple_of(j * FF_BLOCK, FF_BLOCK)
        up = jnp.maximum(_dot(n2, w_up_ref[:, pl.ds(lo, FF_BLOCK)]), 0.0)
        return f + _dot((up * up).astype(BF16), w_down_ref[pl.ds(lo, FF_BLOCK), :])

    f = lax.fori_loop(0, D_FF // FF_BLOCK, block, jnp.zeros_like(h1))
    y_ref[0] = _rmsnorm(h1 + f, norm_final_ref[...])


def _prompt_kernel(x_ref, cos_ref, sin_ref, hist0_ref, state0_ref, *refs, tiles_per_stream, n_tiles):
    t = pl.program_id(0)
    ubuf_ref, state_ref, h1_ref, inner_ref, cross_ref, kdec_ref = refs[-6:]

    @pl.when(t == 0)
    def _():
        h1_ref[...] = jnp.zeros_like(h1_ref)
        for h, (inner, cross, kdec, _) in enumerate(_decay_tables(PROMPT_CHUNK)):
            inner_ref[h] = inner
            cross_ref[h] = cross
            kdec_ref[h] = kdec

    @pl.when(t % tiles_per_stream == 0)
    def _():
        ubuf_ref[0, SUBLANES - 2:SUBLANES, :] = hist0_ref[...]
        state_ref[...] = state0_ref[...]

    pl.when(t < n_tiles)(functools.partial(_prompt_step, x_ref, cos_ref, sin_ref, *refs))
    (_, _, _, _, _, norm_ffn_ref, w_up_ref, w_down_ref, norm_final_ref, y_ref) = refs[:10]
    pl.when(t == n_tiles)(functools.partial(_ffn_drain, norm_ffn_ref, w_up_ref, w_down_ref,
                                            norm_final_ref, y_ref, h1_ref))


def _sample_kernel(x_ref, cos_ref, sin_ref, hist_ref, state_in_ref,
                   norm_mix_ref, w_in_ref, conv_w_ref, gn_w_ref, w_out_ref, norm_ffn_ref,
                   w_up_ref, w_down_ref, norm_final_ref,
                   y_ref, hist_out_ref, state_out_ref, ubuf_ref):
    n, seg = SAMPLE_STREAMS, x_ref.shape[1]
    states = [[state_in_ref[i, h] for h in range(RET_HEADS)] for i in range(n)]
    cos_t = jnp.concatenate([cos_ref[...]] * n, axis=0)
    sin_t = jnp.concatenate([sin_ref[...]] * n, axis=0)
    y, new_hist, new_states = _layer(
        x_ref[...].reshape(n * seg, D_MODEL), hist_ref[...], states, n, seg, seg, ubuf_ref,
        cos_t, sin_t, norm_mix_ref[...], w_in_ref, conv_w_ref[...],
        gn_w_ref[...], w_out_ref, norm_ffn_ref[...], w_up_ref, w_down_ref, norm_final_ref[...])
    y_ref[...] = y.reshape(n, seg, D_MODEL)
    hist_out_ref[...] = new_hist
    for i in range(n):
        for h in range(RET_HEADS):
            state_out_ref[i, h] = new_states[i][h]


def _rope_tables(first_pos, n):
    half = RET_HEAD_DIM // 2
    inv_freq = ROPE_BASE ** (-np.arange(half, dtype=np.float64) / half)
    ang = (first_pos + np.arange(n, dtype=np.float64))[:, None] * inv_freq[None, :]
    cos, sin = np.cos(ang), np.sin(ang)
    return (jnp.asarray(np.concatenate([cos, cos], axis=-1), dtype=F32),
            jnp.asarray(np.concatenate([-sin, sin], axis=-1), dtype=F32))


def _const_spec(shape):
    nd = len(shape)
    return pl.BlockSpec(shape, lambda *_: (0,) * nd, pipeline_mode=pl.Buffered(1))


def kernel(x_prompt, x_sample, state_conv, state_ret, meta_tokens, norm_mix_w, w_in, conv_w,
           ret_gn_w, w_out, norm_ffn_w, w_up, w_down, norm_final_w):
    depth = w_in.shape[0]
    assert depth == 1, "single-layer step"
    batch, seq, _ = x_prompt.shape
    dec_batch, dec_seq, _ = x_sample.shape
    assert seq % PROMPT_TILE == 0 and dec_batch % SAMPLE_STREAMS == 0

    w_in_b, w_out_b = w_in[0].astype(BF16), w_out[0].astype(BF16)
    w_up_b, w_down_b = w_up[0].astype(BF16), w_down[0].astype(BF16)
    norm_mix, norm_ffn = norm_mix_w[0][None], norm_ffn_w[0][None]
    norm_final = norm_final_w[None]
    gn_w, cw = ret_gn_w[0][None], conv_w[0]

    cos_m, sin_m = _rope_tables(-N_META, N_META)
    cos_p, sin_p = _rope_tables(0, seq)
    cos_s, sin_s = _rope_tables(PAST_LEN, dec_seq)

    params = (norm_mix, w_in_b, cw, gn_w, w_out_b, norm_ffn, w_up_b, w_down_b, norm_final)
    param_specs = [_const_spec(p.shape) for p in params]
    d_in = w_in_b.shape[1]

    meta_hist, meta_state = pl.pallas_call(
        _meta_kernel,
        out_shape=(jax.ShapeDtypeStruct((CONV_WIDTH - 1, D_CONV), F32),
                   jax.ShapeDtypeStruct((RET_HEADS, RET_HEAD_DIM, RET_HEAD_DIM), F32)),
        compiler_params=pltpu.CompilerParams(vmem_limit_bytes=VMEM_LIMIT_BYTES),
        name="meta_state",
    )(meta_tokens, cos_m, sin_m, norm_mix, w_in_b)

    tiles_per_stream = seq // PROMPT_TILE
    n_tiles = batch * tiles_per_stream

    def cur(t):
        return jnp.minimum(t, n_tiles - 1)

    def prev(t):
        return jnp.maximum(t - 1, 0)

    y_prompt, conv_p, ret_p = pl.pallas_call(
        functools.partial(_prompt_kernel, tiles_per_stream=tiles_per_stream, n_tiles=n_tiles),
        grid=(n_tiles + 1,),
        in_specs=[
            pl.BlockSpec((1, PROMPT_TILE, D_MODEL),
                         lambda t: (cur(t) // tiles_per_stream, cur(t) % tiles_per_stream, 0)),
            pl.BlockSpec((PROMPT_TILE, RET_HEAD_DIM), lambda t: (cur(t) % tiles_per_stream, 0)),
            pl.BlockSpec((PROMPT_TILE, RET_HEAD_DIM), lambda t: (cur(t) % tiles_per_stream, 0)),
            _const_spec(meta_hist.shape),
            _const_spec(meta_state.shape),
        ] + param_specs,
        out_specs=(
            pl.BlockSpec((1, PROMPT_TILE, D_MODEL),
                         lambda t: (prev(t) // tiles_per_stream, prev(t) % tiles_per_stream, 0)),
            pl.BlockSpec((1, CONV_WIDTH - 1, D_CONV), lambda t: (cur(t) // tiles_per_stream, 0, 0)),
            pl.BlockSpec((1, RET_HEADS, RET_HEAD_DIM, RET_HEAD_DIM),
                         lambda t: (cur(t) // tiles_per_stream, 0, 0, 0)),
        ),
        out_shape=(
            jax.ShapeDtypeStruct((batch, seq, D_MODEL), F32),
            jax.ShapeDtypeStruct((batch, CONV_WIDTH - 1, D_CONV), F32),
            jax.ShapeDtypeStruct((batch, RET_HEADS, RET_HEAD_DIM, RET_HEAD_DIM), F32),
        ),
        scratch_shapes=[
            pltpu.VMEM((1, PROMPT_TILE + SUBLANES, D_CONV), F32),
            pltpu.VMEM((RET_HEADS, RET_HEAD_DIM, RET_HEAD_DIM), F32),
            pltpu.VMEM((PROMPT_TILE, D_MODEL), F32),
            pltpu.VMEM((RET_HEADS, PROMPT_CHUNK, PROMPT_CHUNK), F32),
            pltpu.VMEM((RET_HEADS, PROMPT_CHUNK, RET_HEAD_DIM), F32),
            pltpu.VMEM((RET_HEADS, PROMPT_CHUNK, RET_HEAD_DIM), F32),
        ],
        compiler_params=pltpu.CompilerParams(
            dimension_semantics=("arbitrary",),
            vmem_limit_bytes=VMEM_LIMIT_BYTES),
        name="prompt_step",
    )(x_prompt, cos_p, sin_p, meta_hist, meta_state, *params)

    ns = SAMPLE_STREAMS
    y_sample, conv_s, ret_s = pl.pallas_call(
        _sample_kernel,
        grid=(dec_batch // ns,),
        in_specs=[
            pl.BlockSpec((ns, dec_seq, D_MODEL), lambda b: (b, 0, 0)),
            _const_spec(cos_s.shape),
            _const_spec(sin_s.shape),
            pl.BlockSpec((ns, CONV_WIDTH - 1, D_CONV), lambda b: (b, 0, 0)),
            pl.BlockSpec((ns, RET_HEADS, RET_HEAD_DIM, RET_HEAD_DIM), lambda b: (b, 0, 0, 0)),
        ] + param_specs,
        out_specs=(
            pl.BlockSpec((ns, dec_seq, D_MODEL), lambda b: (b, 0, 0)),
            pl.BlockSpec((ns, CONV_WIDTH - 1, D_CONV), lambda b: (b, 0, 0)),
            pl.BlockSpec((ns, RET_HEADS, RET_HEAD_DIM, RET_HEAD_DIM), lambda b: (b, 0, 0, 0)),
        ),
        out_shape=(
            jax.ShapeDtypeStruct((dec_batch, dec_seq, D_MODEL), F32),
            jax.ShapeDtypeStruct((dec_batch, CONV_WIDTH - 1, D_CONV), F32),
            jax.ShapeDtypeStruct((dec_batch, RET_HEADS, RET_HEAD_DIM, RET_HEAD_DIM), F32),
        ),
        scratch_shapes=[pltpu.VMEM((ns, dec_seq + SUBLANES, D_CONV), F32)],
        compiler_params=pltpu.CompilerParams(
            dimension_semantics=("arbitrary",),
            vmem_limit_bytes=VMEM_LIMIT_BYTES),
        name="sample_step",
    )(x_sample, cos_s, sin_s, state_conv[0], state_ret[0], *params)

    return (y_prompt, y_sample, conv_p[None], ret_p[None], conv_s[None], ret_s[None])
```

```python
import functools
import math

import jax
import jax.numpy as jnp
import numpy as np
from jax import lax
from jax.experimental import pallas as pl
from jax.experimental.pallas import tpu as pltpu

D_MODEL = 1024
N_META = 16
PAST_LEN = 2048
D_CONV = 512
CONV_WIDTH = 3
RET_HEADS = 4
RET_HEAD_DIM = 128
D_RET = RET_HEADS * RET_HEAD_DIM
D_FF = 4 * D_MODEL
ROPE_BASE = 10000.0
NORM_EPS = 1e-6
GN_EPS = 1e-5
K_SCALE = RET_HEAD_DIM ** -0.5
LOG_GAMMA = tuple(math.log(1.0 - 2.0 ** (-5.0 - h)) for h in range(RET_HEADS))

SUBLANES = 8
VMEM_LIMIT_BYTES = 58 * 1024 * 1024

PROMPT_TILE = 512
PROMPT_CHUNK = 256
SAMPLE_STREAMS = 8
FF_BLOCK = 1024

BF16 = jnp.bfloat16
F32 = jnp.float32


def _dot(a, b):
    return jnp.dot(a, b, preferred_element_type=F32)


def _rmsnorm(x, w):
    ms = jnp.mean(x * x, axis=-1, keepdims=True)
    return x * lax.rsqrt(ms + NORM_EPS) * w


def _rotate(x, cos_t, sin_t):
    return x * cos_t + pltpu.roll(x, RET_HEAD_DIM // 2, 1) * sin_t


def _decay_tables(chunk):
    row = lax.broadcasted_iota(jnp.int32, (chunk, chunk), 0)
    col = lax.broadcasted_iota(jnp.int32, (chunk, chunk), 1)
    diff = (row - col).astype(F32)
    rowv = lax.broadcasted_iota(jnp.int32, (chunk, RET_HEAD_DIM), 0).astype(F32)
    tabs = []
    for lg in LOG_GAMMA:
        inner = jnp.where(diff >= 0.0, jnp.exp(jnp.maximum(diff, 0.0) * lg), 0.0) * K_SCALE
        cross = jnp.exp((rowv + 1.0) * lg)
        kdec = jnp.exp((chunk - 1.0 - rowv) * lg) * K_SCALE
        tabs.append((inner, cross, kdec, math.exp(chunk * lg)))
    return tabs


def _mixer_inputs(x, norm_w, w_in_ref, cos_t, sin_t):
    hn = _rmsnorm(x, norm_w).astype(BF16)
    pc = _dot(hn, w_in_ref[:, : 3 * D_CONV])
    xt = pc[:, :D_CONV]
    b_gate = pc[:, D_CONV: 2 * D_CONV]
    c_gate = pc[:, 2 * D_CONV:]
    u = c_gate * xt
    pr = _dot(hn, w_in_ref[:, 3 * D_CONV:])
    q, k, v, g = (pr[:, i * D_RET:(i + 1) * D_RET] for i in range(4))
    qh, kh, vh = [], [], []
    for h in range(RET_HEADS):
        sl = slice(h * RET_HEAD_DIM, (h + 1) * RET_HEAD_DIM)
        qh.append(_rotate(q[:, sl], cos_t, sin_t).astype(BF16))
        kh.append(_rotate(k[:, sl], cos_t, sin_t))
        vh.append(v[:, sl].astype(BF16))
    return u, b_gate, qh, kh, vh, g


def _retention_scores(qc, kc, vc, s, tabs):
    _, cross_decay, k_decay, gamma_l = tabs
    scores = lax.dot_general(qc, kc.astype(BF16), (((1,), (1,)), ((), ())),
                             preferred_element_type=F32)
    cross = _dot(qc, s.astype(BF16)) * cross_decay
    kd = (kc * k_decay).astype(BF16)
    s_new = gamma_l * s + lax.dot_general(kd, vc, (((0,), (0,)), ((), ())),
                                          preferred_element_type=F32)
    return scores, cross, s_new


def _retention_output(scores, cross, vc, tabs):
    p = (scores * tabs[0]).astype(BF16)
    return _dot(p, vc) + cross


def _retention_chunk(qc, kc, vc, s, tabs):
    scores, cross, s_new = _retention_scores(qc, kc, vc, s, tabs)
    return _retention_output(scores, cross, vc, tabs), s_new


def _retention(qh, kh, vh, states, n_streams, seg, chunk, tabs):
    new_states = [list(st) for st in states]
    groups = [(n, c) for n in range(n_streams) for c in range(seg // chunk)]

    def first_stage(n, c):
        lo = n * seg + c * chunk
        return [_retention_scores(qh[h][lo:lo + chunk], kh[h][lo:lo + chunk], vh[h][lo:lo + chunk],
                                  new_states[n][h], tabs[h]) for h in range(RET_HEADS)]

    rows = []
    staged = first_stage(*groups[0])
    for i, (n, c) in enumerate(groups):
        lo = n * seg + c * chunk
        for h in range(RET_HEADS):
            new_states[n][h] = staged[h][2]
        this = staged
        if i + 1 < len(groups):
            staged = first_stage(*groups[i + 1])
        rows.append([_retention_output(this[h][0], this[h][1], vh[h][lo:lo + chunk], tabs[h])
                     for h in range(RET_HEADS)])
    o_heads = [jnp.concatenate([r[h] for r in rows], axis=0) if len(rows) > 1 else rows[0][h]
               for h in range(RET_HEADS)]
    return o_heads, new_states


def _group_norm_gate(o_heads, g, gn_w):
    outs = []
    for h in range(RET_HEADS):
        o = o_heads[h]
        mu = jnp.mean(o, axis=-1, keepdims=True)
        d = o - mu
        var = jnp.mean(d * d, axis=-1, keepdims=True)
        sl = slice(h * RET_HEAD_DIM, (h + 1) * RET_HEAD_DIM)
        gh = g[:, sl]
        outs.append((gh * jax.nn.sigmoid(gh)) * (d * lax.rsqrt(var + GN_EPS) * gn_w[:, sl]))
    return jnp.concatenate(outs, axis=1).astype(BF16)


def _ffn_up(n2, w_up_ref, j):
    up = jnp.maximum(_dot(n2, w_up_ref[:, j * FF_BLOCK:(j + 1) * FF_BLOCK]), 0.0)
    return (up * up).astype(BF16)


def _ffn_down(hid, w_down_ref, j):
    return _dot(hid, w_down_ref[j * FF_BLOCK:(j + 1) * FF_BLOCK, :])


def _ffn_and_final(h1, norm_ffn_w, w_up_ref, w_down_ref, norm_final_w):
    n2 = _rmsnorm(h1, norm_ffn_w).astype(BF16)
    f = None
    for j in range(D_FF // FF_BLOCK):
        d = _ffn_down(_ffn_up(n2, w_up_ref, j), w_down_ref, j)
        f = d if f is None else f + d
    return _rmsnorm(h1 + f, norm_final_w)


def _conv_group(u, b_gate, conv_w, ubuf_ref, n_streams, seg):
    ubuf_ref[:, SUBLANES:SUBLANES + seg, :] = u.reshape(n_streams, seg, D_CONV)
    u1 = ubuf_ref[:, SUBLANES - 1:SUBLANES - 1 + seg, :].reshape(n_streams * seg, D_CONV)
    u2 = ubuf_ref[:, SUBLANES - 2:SUBLANES - 2 + seg, :].reshape(n_streams * seg, D_CONV)
    conv_y = conv_w[0:1, :] * u2 + conv_w[1:2, :] * u1 + conv_w[2:3, :] * u
    new_hist = ubuf_ref[:, SUBLANES + seg - 2:SUBLANES + seg, :]
    return (b_gate * conv_y).astype(BF16), new_hist


def _layer(x, hist, states, n_streams, seg, chunk, ubuf_ref, cos_t, sin_t,
           norm_mix_w, w_in_ref, conv_w, gn_w, w_out_ref, norm_ffn_w, w_up_ref, w_down_ref,
           norm_final_w):
    u, b_gate, qh, kh, vh, g = _mixer_inputs(x, norm_mix_w, w_in_ref, cos_t, sin_t)
    ubuf_ref[:, SUBLANES - 2:SUBLANES, :] = hist
    conv_out, new_hist = _conv_group(u, b_gate, conv_w, ubuf_ref, n_streams, seg)
    tabs = _decay_tables(chunk)
    o_heads, new_states = _retention(qh, kh, vh, states, n_streams, seg, chunk, tabs)
    ret_out = _group_norm_gate(o_heads, g, gn_w)
    h1 = x + (_dot(conv_out, w_out_ref[:D_CONV, :]) + _dot(ret_out, w_out_ref[D_CONV:, :]))
    y = _ffn_and_final(h1, norm_ffn_w, w_up_ref, w_down_ref, norm_final_w)
    return y, new_hist, new_states


def _meta_kernel(meta_ref, cos_ref, sin_ref, norm_mix_ref, w_in_ref, hist_out_ref, state_out_ref):
    u, _, qh, kh, vh, _ = _mixer_inputs(meta_ref[...], norm_mix_ref[...], w_in_ref,
                                        cos_ref[...], sin_ref[...])
    hist_out_ref[...] = u[N_META - 2:N_META, :]
    tabs = _decay_tables(N_META)
    for h in range(RET_HEADS):
        _, _, k_decay, _ = tabs[h]
        kd = (kh[h] * k_decay).astype(BF16)
        state_out_ref[h] = lax.dot_general(kd, vh[h], (((0,), (0,)), ((), ())),
                                           preferred_element_type=F32)


def _prompt_step(x_ref, cos_ref, sin_ref, norm_mix_ref, w_in_ref, conv_w_ref, gn_w_ref, w_out_ref,
                 norm_ffn_ref, w_up_ref, w_down_ref, norm_final_ref,
                 y_ref, hist_out_ref, state_out_ref, ubuf_ref, state_ref, h1_ref,
                 inner_ref, cross_ref, kdec_ref):
    n_chunks = PROMPT_TILE // PROMPT_CHUNK
    n_blocks = D_FF // FF_BLOCK
    assert n_chunks * RET_HEADS == 2 * n_blocks
    half = FF_BLOCK // 2

    x = x_ref[0]
    u, b_gate, qh, kh, vh, g = _mixer_inputs(x, norm_mix_ref[...], w_in_ref,
                                             cos_ref[...], sin_ref[...])
    conv_out, new_hist = _conv_group(u, b_gate, conv_w_ref[...], ubuf_ref, 1, PROMPT_TILE)
    tabs = [(inner_ref[h], cross_ref[h], kdec_ref[h], math.exp(PROMPT_CHUNK * LOG_GAMMA[h]))
            for h in range(RET_HEADS)]
    st = [state_ref[h] for h in range(RET_HEADS)]
    o_chunks = [[None] * n_chunks for _ in range(RET_HEADS)]
    h1_prev = h1_ref[...]
    n2 = _rmsnorm(h1_prev, norm_ffn_ref[...]).astype(BF16)
    carry = {"f": None, "hid": None, "hid_lo": None}

    def up_lo():
        up = jnp.maximum(_dot(n2, w_up_ref[:, :half]), 0.0)
        carry["hid_lo"] = (up * up).astype(BF16)

    def up_hi():
        up = jnp.maximum(_dot(n2, w_up_ref[:, half:FF_BLOCK]), 0.0)
        carry["hid"] = jnp.concatenate([carry["hid_lo"], (up * up).astype(BF16)], axis=1)

    def up(j):
        carry["hid"] = _ffn_up(n2, w_up_ref, j)

    def down(j):
        d = _ffn_down(carry["hid"], w_down_ref, j)
        carry["f"] = d if carry["f"] is None else carry["f"] + d

    items = [up_lo, up_hi, functools.partial(down, 0)]
    for j in range(1, n_blocks):
        items += [functools.partial(up, j), functools.partial(down, j)]
    assert len(items) == n_chunks * RET_HEADS + 1
    items.reverse()

    items.pop()()
    for c in range(n_chunks):
        rows = slice(c * PROMPT_CHUNK, (c + 1) * PROMPT_CHUNK)
        for h in range(RET_HEADS):
            o_chunks[h][c], st[h] = _retention_chunk(qh[h][rows], kh[h][rows], vh[h][rows],
                                                     st[h], tabs[h])
            items.pop()()
    y_ref[0] = _rmsnorm(h1_prev + carry["f"], norm_final_ref[...])

    o_heads = [jnp.concatenate(o_chunks[h], axis=0) for h in range(RET_HEADS)]
    ret_out = _group_norm_gate(o_heads, g, gn_w_ref[...])
    h1_ref[...] = x + (_dot(conv_out, w_out_ref[:D_CONV, :]) + _dot(ret_out, w_out_ref[D_CONV:, :]))
    ubuf_ref[:, SUBLANES - 2:SUBLANES, :] = new_hist
    hist_out_ref[...] = new_hist
    for h in range(RET_HEADS):
        state_ref[h] = st[h]
        state_out_ref[0, h] = st[h]


def _ffn_drain(norm_ffn_ref, w_up_ref, w_down_ref, norm_final_ref, y_ref, h1_ref):
    h1 = h1_ref[...]
    n2 = _rmsnorm(h1, norm_ffn_ref[...]).astype(BF16)

    def block(j, f):
        lo = pl.multiple_of(j * FF_BLOCK, FF_BLOCK)
        up = jnp.maximum(_dot(n2, w_up_ref[:, pl.ds(lo, FF_BLOCK)]), 0.0)
        return f + _dot((up * up).astype(BF16), w_down_ref[pl.ds(lo, FF_BLOCK), :])

    f = lax.fori_loop(0, D_FF // FF_BLOCK, block, jnp.zeros_like(h1))
    y_ref[0] = _rmsnorm(h1 + f, norm_final_ref[...])


def _prompt_kernel(x_ref, cos_ref, sin_ref, hist0_ref, state0_ref,
                   norm_mix_ref, w_in_ref, conv_w_ref, gn_w_ref, w_out_ref, norm_ffn_ref,
                   w_up_ref, w_down_ref, norm_final_ref,
                   y_ref, hist_out_ref, state_out_ref,
                   ubuf_ref, state_ref, h1_ref, inner_ref, cross_ref, kdec_ref,
                   *, tiles_per_stream, n_tiles):
    t = pl.program_id(0)

    @pl.when(t == 0)
    def _():
        h1_ref[...] = jnp.zeros_like(h1_ref)
        for h, (inner, cross, kdec, _) in enumerate(_decay_tables(PROMPT_CHUNK)):
            inner_ref[h] = inner
            cross_ref[h] = cross
            kdec_ref[h] = kdec

    @pl.when(t % tiles_per_stream == 0)
    def _():
        ubuf_ref[0, SUBLANES - 2:SUBLANES, :] = hist0_ref[...]
        state_ref[...] = state0_ref[...]

    pl.when(t < n_tiles)(functools.partial(
        _prompt_step, x_ref, cos_ref, sin_ref, norm_mix_ref, w_in_ref, conv_w_ref, gn_w_ref,
        w_out_ref, norm_ffn_ref, w_up_ref, w_down_ref, norm_final_ref,
        y_ref, hist_out_ref, state_out_ref, ubuf_ref, state_ref, h1_ref,
        inner_ref, cross_ref, kdec_ref))
    pl.when(t == n_tiles)(functools.partial(
        _ffn_drain, norm_ffn_ref, w_up_ref, w_down_ref, norm_final_ref, y_ref, h1_ref))


def _sample_kernel(x_ref, cos_ref, sin_ref, hist_ref, state_in_ref,
                   norm_mix_ref, w_in_ref, conv_w_ref, gn_w_ref, w_out_ref, norm_ffn_ref,
                   w_up_ref, w_down_ref, norm_final_ref,
                   y_ref, hist_out_ref, state_out_ref, ubuf_ref):
    n, seg = SAMPLE_STREAMS, x_ref.shape[1]
    states = [[state_in_ref[i, h] for h in range(RET_HEADS)] for i in range(n)]
    cos_t = jnp.concatenate([cos_ref[...]] * n, axis=0)
    sin_t = jnp.concatenate([sin_ref[...]] * n, axis=0)
    y, new_hist, new_states = _layer(
        x_ref[...].reshape(n * seg, D_MODEL), hist_ref[...], states, n, seg, seg, ubuf_ref,
        cos_t, sin_t, norm_mix_ref[...], w_in_ref, conv_w_ref[...],
        gn_w_ref[...], w_out_ref, norm_ffn_ref[...], w_up_ref, w_down_ref, norm_final_ref[...])
    y_ref[...] = y.reshape(n, seg, D_MODEL)
    hist_out_ref[...] = new_hist
    for i in range(n):
        for h in range(RET_HEADS):
            state_out_ref[i, h] = new_states[i][h]


def _rope_tables(first_pos, n):
    half = RET_HEAD_DIM // 2
    inv_freq = ROPE_BASE ** (-np.arange(half, dtype=np.float64) / half)
    ang = (first_pos + np.arange(n, dtype=np.float64))[:, None] * inv_freq[None, :]
    cos, sin = np.cos(ang), np.sin(ang)
    return (jnp.asarray(np.concatenate([cos, cos], axis=-1), dtype=F32),
            jnp.asarray(np.concatenate([-sin, sin], axis=-1), dtype=F32))


def _const_spec(shape):
    nd = len(shape)
    return pl.BlockSpec(shape, lambda *_: (0,) * nd, pipeline_mode=pl.Buffered(1))


def kernel(x_prompt, x_sample, state_conv, state_ret, meta_tokens, norm_mix_w, w_in, conv_w,
           ret_gn_w, w_out, norm_ffn_w, w_up, w_down, norm_final_w):
    depth = w_in.shape[0]
    assert depth == 1, "single-layer step"
    batch, seq, _ = x_prompt.shape
    dec_batch, dec_seq, _ = x_sample.shape
    assert seq % PROMPT_TILE == 0 and dec_batch % SAMPLE_STREAMS == 0

    w_in_b, w_out_b = w_in[0].astype(BF16), w_out[0].astype(BF16)
    w_up_b, w_down_b = w_up[0].astype(BF16), w_down[0].astype(BF16)
    norm_mix, norm_ffn = norm_mix_w[0][None], norm_ffn_w[0][None]
    norm_final = norm_final_w[None]
    gn_w, cw = ret_gn_w[0][None], conv_w[0]

    cos_m, sin_m = _rope_tables(-N_META, N_META)
    cos_p, sin_p = _rope_tables(0, seq)
    cos_s, sin_s = _rope_tables(PAST_LEN, dec_seq)

    params = (norm_mix, w_in_b, cw, gn_w, w_out_b, norm_ffn, w_up_b, w_down_b, norm_final)
    param_specs = [_const_spec(p.shape) for p in params]

    meta_hist, meta_state = pl.pallas_call(
        _meta_kernel,
        out_shape=(jax.ShapeDtypeStruct((CONV_WIDTH - 1, D_CONV), F32),
                   jax.ShapeDtypeStruct((RET_HEADS, RET_HEAD_DIM, RET_HEAD_DIM), F32)),
        compiler_params=pltpu.CompilerParams(vmem_limit_bytes=VMEM_LIMIT_BYTES),
        name="meta_state",
    )(meta_tokens, cos_m, sin_m, norm_mix, w_in_b)

    tiles_per_stream = seq // PROMPT_TILE
    n_tiles = batch * tiles_per_stream

    def cur(t):
        return jnp.minimum(t, n_tiles - 1)

    def prev(t):
        return jnp.maximum(t - 1, 0)

    y_prompt, conv_p, ret_p = pl.pallas_call(
        functools.partial(_prompt_kernel, tiles_per_stream=tiles_per_stream, n_tiles=n_tiles),
        grid=(n_tiles + 1,),
        in_specs=[
            pl.BlockSpec((1, PROMPT_TILE, D_MODEL),
                         lambda t: (cur(t) // tiles_per_stream, cur(t) % tiles_per_stream, 0)),
            pl.BlockSpec((PROMPT_TILE, RET_HEAD_DIM), lambda t: (cur(t) % tiles_per_stream, 0)),
            pl.BlockSpec((PROMPT_TILE, RET_HEAD_DIM), lambda t: (cur(t) % tiles_per_stream, 0)),
            _const_spec(meta_hist.shape),
            _const_spec(meta_state.shape),
        ] + param_specs,
        out_specs=(
            pl.BlockSpec((1, PROMPT_TILE, D_MODEL),
                         lambda t: (prev(t) // tiles_per_stream, prev(t) % tiles_per_stream, 0)),
            pl.BlockSpec((1, CONV_WIDTH - 1, D_CONV), lambda t: (cur(t) // tiles_per_stream, 0, 0)),
            pl.BlockSpec((1, RET_HEADS, RET_HEAD_DIM, RET_HEAD_DIM),
                         lambda t: (cur(t) // tiles_per_stream, 0, 0, 0)),
        ),
        out_shape=(
            jax.ShapeDtypeStruct((batch, seq, D_MODEL), F32),
            jax.ShapeDtypeStruct((batch, CONV_WIDTH - 1, D_CONV), F32),
            jax.ShapeDtypeStruct((batch, RET_HEADS, RET_HEAD_DIM, RET_HEAD_DIM), F32),
        ),
        scratch_shapes=[
            pltpu.VMEM((1, PROMPT_TILE + SUBLANES, D_CONV), F32),
            pltpu.VMEM((RET_HEADS, RET_HEAD_DIM, RET_HEAD_DIM), F32),
            pltpu.VMEM((PROMPT_TILE, D_MODEL), F32),
            pltpu.VMEM((RET_HEADS, PROMPT_CHUNK, PROMPT_CHUNK), F32),
            pltpu.VMEM((RET_HEADS, PROMPT_CHUNK, RET_HEAD_DIM), F32),
            pltpu.VMEM((RET_HEADS, PROMPT_CHUNK, RET_HEAD_DIM), F32),
        ],
        compiler_params=pltpu.CompilerParams(
            dimension_semantics=("arbitrary",),
            vmem_limit_bytes=VMEM_LIMIT_BYTES),
        name="prompt_step",
    )(x_prompt, cos_p, sin_p, meta_hist, meta_state, *params)

    ns = SAMPLE_STREAMS
    y_sample, conv_s, ret_s = pl.pallas_call(
        _sample_kernel,
        grid=(dec_batch // ns,),
        in_specs=[
            pl.BlockSpec((ns, dec_seq, D_MODEL), lambda b: (b, 0, 0)),
            _const_spec(cos_s.shape),
            _const_spec(sin_s.shape),
            pl.BlockSpec((ns, CONV_WIDTH - 1, D_CONV), lambda b: (b, 0, 0)),
            pl.BlockSpec((ns, RET_HEADS, RET_HEAD_DIM, RET_HEAD_DIM), lambda b: (b, 0, 0, 0)),
        ] + param_specs,
        out_specs=(
            pl.BlockSpec((ns, dec_seq, D_MODEL), lambda b: (b, 0, 0)),
            pl.BlockSpec((ns, CONV_WIDTH - 1, D_CONV), lambda b: (b, 0, 0)),
            pl.BlockSpec((ns, RET_HEADS, RET_HEAD_DIM, RET_HEAD_DIM), lambda b: (b, 0, 0, 0)),
        ),
        out_shape=(
            jax.ShapeDtypeStruct((dec_batch, dec_seq, D_MODEL), F32),
            jax.ShapeDtypeStruct((dec_batch, CONV_WIDTH - 1, D_CONV), F32),
            jax.ShapeDtypeStruct((dec_batch, RET_HEADS, RET_HEAD_DIM, RET_HEAD_DIM), F32),
        ),
        scratch_shapes=[pltpu.VMEM((ns, dec_seq + SUBLANES, D_CONV), F32)],
        compiler_params=pltpu.CompilerParams(
            dimension_semantics=("arbitrary",),
            vmem_limit_bytes=VMEM_LIMIT_BYTES),
        name="sample_step",
    )(x_sample, cos_s, sin_s, state_conv[0], state_ret[0], *params)

    return (y_prompt, y_sample, conv_p[None], ret_p[None], conv_s[None], ret_s[None])
```

```python
import functools
import math

import jax
import jax.numpy as jnp
import numpy as np
from jax import lax
from jax.experimental import pallas as pl
from jax.experimental.pallas import tpu as pltpu

D_MODEL = 1024
N_META = 16
PAST_LEN = 2048
D_CONV = 512
CONV_WIDTH = 3
RET_HEADS = 4
RET_HEAD_DIM = 128
D_RET = RET_HEADS * RET_HEAD_DIM
D_FF = 4 * D_MODEL
ROPE_BASE = 10000.0
NORM_EPS = 1e-6
GN_EPS = 1e-5
K_SCALE = RET_HEAD_DIM ** -0.5
LOG_GAMMA = tuple(math.log(1.0 - 2.0 ** (-5.0 - h)) for h in range(RET_HEADS))

SUBLANES = 8
VMEM_LIMIT_BYTES = 58 * 1024 * 1024

PROMPT_TILE = 512
PROMPT_CHUNK = 256
SAMPLE_STREAMS = 8
FF_BLOCK = 1024

BF16 = jnp.bfloat16
F32 = jnp.float32


def _dot(a, b):
    return jnp.dot(a, b, preferred_element_type=F32)


def _rmsnorm(x, w):
    ms = jnp.mean(x * x, axis=-1, keepdims=True)
    return x * lax.rsqrt(ms + NORM_EPS) * w


def _rotate(x, rope):
    return x * rope[:, :RET_HEAD_DIM] + pltpu.roll(x, RET_HEAD_DIM // 2, 1) * rope[:, RET_HEAD_DIM:]


def _decay_tables(chunk):
    row = lax.broadcasted_iota(jnp.int32, (chunk, chunk), 0)
    col = lax.broadcasted_iota(jnp.int32, (chunk, chunk), 1)
    diff = (row - col).astype(F32)
    rowv = lax.broadcasted_iota(jnp.int32, (chunk, RET_HEAD_DIM), 0).astype(F32)
    tabs = []
    for lg in LOG_GAMMA:
        inner = jnp.where(diff >= 0.0, jnp.exp(jnp.maximum(diff, 0.0) * lg), 0.0) * K_SCALE
        cross = jnp.exp((rowv + 1.0) * lg)
        kdec = jnp.exp((chunk - 1.0 - rowv) * lg) * K_SCALE
        tabs.append((inner, cross, kdec, math.exp(chunk * lg)))
    return tabs


def _mixer_inputs(x, norm_w, w_in_ref, rope):
    hn = _rmsnorm(x, norm_w).astype(BF16)
    pc = _dot(hn, w_in_ref[:, : 3 * D_CONV])
    xt = pc[:, :D_CONV]
    b_gate = pc[:, D_CONV: 2 * D_CONV]
    c_gate = pc[:, 2 * D_CONV:]
    u = c_gate * xt
    pr = _dot(hn, w_in_ref[:, 3 * D_CONV:])
    q, k, v, g = (pr[:, i * D_RET:(i + 1) * D_RET] for i in range(4))
    qh, kh, vh = [], [], []
    for h in range(RET_HEADS):
        sl = slice(h * RET_HEAD_DIM, (h + 1) * RET_HEAD_DIM)
        qh.append(_rotate(q[:, sl], rope).astype(BF16))
        kh.append(_rotate(k[:, sl], rope))
        vh.append(v[:, sl].astype(BF16))
    return u, b_gate, qh, kh, vh, g


def _retention_scores(qc, kc, vc, s, tabs):
    _, cross_decay, k_decay, gamma_l = tabs
    scores = lax.dot_general(qc, kc.astype(BF16), (((1,), (1,)), ((), ())),
                             preferred_element_type=F32)
    cross = _dot(qc, s.astype(BF16)) * cross_decay
    kd = (kc * k_decay).astype(BF16)
    s_new = gamma_l * s + lax.dot_general(kd, vc, (((0,), (0,)), ((), ())),
                                          preferred_element_type=F32)
    return scores, cross, s_new


def _retention_output(scores, cross, vc, tabs):
    p = (scores * tabs[0]).astype(BF16)
    return _dot(p, vc) + cross


def _retention_chunk(qc, kc, vc, s, tabs):
    scores, cross, s_new = _retention_scores(qc, kc, vc, s, tabs)
    return _retention_output(scores, cross, vc, tabs), s_new


def _retention(qh, kh, vh, states, n_streams, seg, chunk, tabs):
    new_states = [list(st) for st in states]
    groups = [(n, c) for n in range(n_streams) for c in range(seg // chunk)]

    def first_stage(n, c):
        lo = n * seg + c * chunk
        return [_retention_scores(qh[h][lo:lo + chunk], kh[h][lo:lo + chunk], vh[h][lo:lo + chunk],
                                  new_states[n][h], tabs[h]) for h in range(RET_HEADS)]

    rows = []
    staged = first_stage(*groups[0])
    for i, (n, c) in enumerate(groups):
        lo = n * seg + c * chunk
        for h in range(RET_HEADS):
            new_states[n][h] = staged[h][2]
        this = staged
        if i + 1 < len(groups):
            staged = first_stage(*groups[i + 1])
        rows.append([_retention_output(this[h][0], this[h][1], vh[h][lo:lo + chunk], tabs[h])
                     for h in range(RET_HEADS)])
    o_heads = [jnp.concatenate([r[h] for r in rows], axis=0) if len(rows) > 1 else rows[0][h]
               for h in range(RET_HEADS)]
    return o_heads, new_states


def _group_norm_gate(o_heads, g, gn_w):
    outs = []
    for h in range(RET_HEADS):
        o = o_heads[h]
        mu = jnp.mean(o, axis=-1, keepdims=True)
        d = o - mu
        var = jnp.mean(d * d, axis=-1, keepdims=True)
        sl = slice(h * RET_HEAD_DIM, (h + 1) * RET_HEAD_DIM)
        gh = g[:, sl]
        outs.append((gh * jax.nn.sigmoid(gh)) * (d * lax.rsqrt(var + GN_EPS) * gn_w[:, sl]))
    return jnp.concatenate(outs, axis=1).astype(BF16)


def _ffn_up(n2, w_up_ref, j):
    up = jnp.maximum(_dot(n2, w_up_ref[:, j * FF_BLOCK:(j + 1) * FF_BLOCK]), 0.0)
    return (up * up).astype(BF16)


def _ffn_down(hid, w_down_ref, j):
    return _dot(hid, w_down_ref[j * FF_BLOCK:(j + 1) * FF_BLOCK, :])


def _ffn_and_final(h1, norm_ffn_w, w_up_ref, w_down_ref, norm_final_w):
    n2 = _rmsnorm(h1, norm_ffn_w).astype(BF16)
    f = None
    for j in range(D_FF // FF_BLOCK):
        d = _ffn_down(_ffn_up(n2, w_up_ref, j), w_down_ref, j)
        f = d if f is None else f + d
    return _rmsnorm(h1 + f, norm_final_w)


def _conv_group(u, b_gate, conv_w, ubuf_ref, n_streams, seg):
    ubuf_ref[:, SUBLANES:SUBLANES + seg, :] = u.reshape(n_streams, seg, D_CONV)
    u1 = ubuf_ref[:, SUBLANES - 1:SUBLANES - 1 + seg, :].reshape(n_streams * seg, D_CONV)
    u2 = ubuf_ref[:, SUBLANES - 2:SUBLANES - 2 + seg, :].reshape(n_streams * seg, D_CONV)
    conv_y = conv_w[0:1, :] * u2 + conv_w[1:2, :] * u1 + conv_w[2:3, :] * u
    new_hist = ubuf_ref[:, SUBLANES + seg - 2:SUBLANES + seg, :]
    return (b_gate * conv_y).astype(BF16), new_hist


def _mixer(x, hist, states, n_streams, seg, chunk, ubuf_ref, rope,
           norm_mix_w, w_in_ref, conv_w, gn_w, w_out_ref):
    u, b_gate, qh, kh, vh, g = _mixer_inputs(x, norm_mix_w, w_in_ref, rope)
    ubuf_ref[:, SUBLANES - 2:SUBLANES, :] = hist
    conv_out, new_hist = _conv_group(u, b_gate, conv_w, ubuf_ref, n_streams, seg)
    tabs = _decay_tables(chunk)
    o_heads, new_states = _retention(qh, kh, vh, states, n_streams, seg, chunk, tabs)
    ret_out = _group_norm_gate(o_heads, g, gn_w)
    h1 = x + (_dot(conv_out, w_out_ref[:D_CONV, :]) + _dot(ret_out, w_out_ref[D_CONV:, :]))
    return h1, new_hist, new_states


def _meta_kernel(meta_ref, rope_ref, norm_mix_ref, w_in_f32_ref,
                 hist_out_ref, state_out_ref, w_in_ref):
    w_in_ref[...] = w_in_f32_ref[...].astype(BF16)
    u, _, qh, kh, vh, _ = _mixer_inputs(meta_ref[...], norm_mix_ref[...], w_in_ref, rope_ref[...])
    hist_out_ref[...] = u[N_META - 2:N_META, :]
    tabs = _decay_tables(N_META)
    for h in range(RET_HEADS):
        _, _, k_decay, _ = tabs[h]
        kd = (kh[h] * k_decay).astype(BF16)
        state_out_ref[h] = lax.dot_general(kd, vh[h], (((0,), (0,)), ((), ())),
                                           preferred_element_type=F32)


def _prompt_step(x_ref, rope_ref, norm_mix_ref, w_in_ref, conv_w_ref, gn_w_ref, w_out_ref,
                 norm_ffn_ref, w_up_ref, w_down_ref, norm_final_ref,
                 y_ref, hist_out_ref, state_out_ref, ubuf_ref, state_ref, h1_ref,
                 inner_ref, cross_ref, kdec_ref):
    n_chunks = PROMPT_TILE // PROMPT_CHUNK
    n_blocks = D_FF // FF_BLOCK
    assert n_chunks * RET_HEADS == 2 * n_blocks
    half = FF_BLOCK // 2

    x = x_ref[0]
    u, b_gate, qh, kh, vh, g = _mixer_inputs(x, norm_mix_ref[...], w_in_ref, rope_ref[...])
    conv_out, new_hist = _conv_group(u, b_gate, conv_w_ref[...], ubuf_ref, 1, PROMPT_TILE)
    tabs = [(inner_ref[h], cross_ref[h], kdec_ref[h], math.exp(PROMPT_CHUNK * LOG_GAMMA[h]))
            for h in range(RET_HEADS)]
    st = [state_ref[h] for h in range(RET_HEADS)]
    o_chunks = [[None] * n_chunks for _ in range(RET_HEADS)]
    h1_prev = h1_ref[...]
    n2 = _rmsnorm(h1_prev, norm_ffn_ref[...]).astype(BF16)
    carry = {"f": None, "hid": None, "hid_lo": None}

    def up_lo():
        up = jnp.maximum(_dot(n2, w_up_ref[:, :half]), 0.0)
        carry["hid_lo"] = (up * up).astype(BF16)

    def up_hi():
        up = jnp.maximum(_dot(n2, w_up_ref[:, half:FF_BLOCK]), 0.0)
        carry["hid"] = jnp.concatenate([carry["hid_lo"], (up * up).astype(BF16)], axis=1)

    def up(j):
        carry["hid"] = _ffn_up(n2, w_up_ref, j)

    def down(j):
        d = _ffn_down(carry["hid"], w_down_ref, j)
        carry["f"] = d if carry["f"] is None else carry["f"] + d

    items = [up_lo, up_hi, functools.partial(down, 0)]
    for j in range(1, n_blocks):
        items += [functools.partial(up, j), functools.partial(down, j)]
    assert len(items) == n_chunks * RET_HEADS + 1
    items.reverse()

    items.pop()()
    for c in range(n_chunks):
        rows = slice(c * PROMPT_CHUNK, (c + 1) * PROMPT_CHUNK)
        for h in range(RET_HEADS):
            o_chunks[h][c], st[h] = _retention_chunk(qh[h][rows], kh[h][rows], vh[h][rows],
                                                     st[h], tabs[h])
            items.pop()()
    y_ref[0] = _rmsnorm(h1_prev + carry["f"], norm_final_ref[...])

    o_heads = [jnp.concatenate(o_chunks[h], axis=0) for h in range(RET_HEADS)]
    ret_out = _group_norm_gate(o_heads, g, gn_w_ref[...])
    h1_ref[...] = x + (_dot(conv_out, w_out_ref[:D_CONV, :]) + _dot(ret_out, w_out_ref[D_CONV:, :]))
    ubuf_ref[:, SUBLANES - 2:SUBLANES, :] = new_hist
    hist_out_ref[...] = new_hist
    for h in range(RET_HEADS):
        state_ref[h] = st[h]
        state_out_ref[0, h] = st[h]


def _ffn_drain(norm_ffn_ref, w_up_ref, w_down_ref, norm_final_ref, y_ref, h1_ref):
    h1 = h1_ref[...]
    n2 = _rmsnorm(h1, norm_ffn_ref[...]).astype(BF16)

    def block(j, f):
        lo = pl.multiple_of(j * FF_BLOCK, FF_BLOCK)
        up = jnp.maximum(_dot(n2, w_up_ref[:, pl.ds(lo, FF_BLOCK)]), 0.0)
        return f + _dot((up * up).astype(BF16), w_down_ref[pl.ds(lo, FF_BLOCK), :])

    f = lax.fori_loop(0, D_FF // FF_BLOCK, block, jnp.zeros_like(h1))
    y_ref[0] = _rmsnorm(h1 + f, norm_final_ref[...])


def _prompt_kernel(x_ref, rope_ref, hist0_ref, state0_ref,
                   norm_mix_ref, w_in_ref, conv_w_ref, gn_w_ref, w_out_ref, norm_ffn_ref,
                   w_up_ref, w_down_ref, norm_final_ref,
                   y_ref, hist_out_ref, state_out_ref,
                   ubuf_ref, state_ref, h1_ref, inner_ref, cross_ref, kdec_ref,
                   *, tiles_per_stream, n_tiles):
    t = pl.program_id(0)

    @pl.when(t == 0)
    def _():
        h1_ref[...] = jnp.zeros_like(h1_ref)
        for h, (inner, cross, kdec, _) in enumerate(_decay_tables(PROMPT_CHUNK)):
            inner_ref[h] = inner
            cross_ref[h] = cross
            kdec_ref[h] = kdec

    @pl.when(t % tiles_per_stream == 0)
    def _():
        ubuf_ref[0, SUBLANES - 2:SUBLANES, :] = hist0_ref[...]
        state_ref[...] = state0_ref[...]

    pl.when(t < n_tiles)(functools.partial(
        _prompt_step, x_ref, rope_ref, norm_mix_ref, w_in_ref, conv_w_ref, gn_w_ref,
        w_out_ref, norm_ffn_ref, w_up_ref, w_down_ref, norm_final_ref,
        y_ref, hist_out_ref, state_out_ref, ubuf_ref, state_ref, h1_ref,
        inner_ref, cross_ref, kdec_ref))
    pl.when(t == n_tiles)(functools.partial(
        _ffn_drain, norm_ffn_ref, w_up_ref, w_down_ref, norm_final_ref, y_ref, h1_ref))


def _sample_kernel(x_ref, rope_ref, hist_ref, state_in_ref,
                   norm_mix_ref, w_in_ref, conv_w_ref, gn_w_ref, w_out_ref, norm_ffn_ref,
                   w_up_hbm_ref, w_down_hbm_ref, norm_final_ref,
                   y_ref, hist_out_ref, state_out_ref, ubuf_ref, w_up_ref, w_down_ref, weight_sem):
    n, seg = SAMPLE_STREAMS, x_ref.shape[1]
    first_step = pl.program_id(0) == 0
    weight_copies = (pltpu.make_async_copy(w_up_hbm_ref, w_up_ref, weight_sem.at[0]),
                     pltpu.make_async_copy(w_down_hbm_ref, w_down_ref, weight_sem.at[1]))

    @pl.when(first_step)
    def _():
        for copy in weight_copies:
            copy.start()

    states = [[state_in_ref[i, h] for h in range(RET_HEADS)] for i in range(n)]
    rope = jnp.concatenate([rope_ref[...]] * n, axis=0)
    h1, new_hist, new_states = _mixer(
        x_ref[...].reshape(n * seg, D_MODEL), hist_ref[...], states, n, seg, seg, ubuf_ref,
        rope, norm_mix_ref[...], w_in_ref, conv_w_ref[...], gn_w_ref[...], w_out_ref)

    @pl.when(first_step)
    def _():
        for copy in weight_copies:
            copy.wait()

    y = _ffn_and_final(h1, norm_ffn_ref[...], w_up_ref, w_down_ref, norm_final_ref[...])
    y_ref[...] = y.reshape(n, seg, D_MODEL)
    hist_out_ref[...] = new_hist
    for i in range(n):
        for h in range(RET_HEADS):
            state_out_ref[i, h] = new_states[i][h]


def _rope_table(first_pos, n):
    half = RET_HEAD_DIM // 2
    inv_freq = ROPE_BASE ** (-np.arange(half, dtype=np.float64) / half)
    ang = (first_pos + np.arange(n, dtype=np.float64))[:, None] * inv_freq[None, :]
    cos, sin = np.cos(ang), np.sin(ang)
    return jnp.asarray(np.concatenate([cos, cos, -sin, sin], axis=-1), dtype=F32)


def _const_spec(shape):
    nd = len(shape)
    return pl.BlockSpec(shape, lambda *_: (0,) * nd, pipeline_mode=pl.Buffered(1))


def kernel(x_prompt, x_sample, state_conv, state_ret, meta_tokens, norm_mix_w, w_in, conv_w,
           ret_gn_w, w_out, norm_ffn_w, w_up, w_down, norm_final_w):
    depth = w_in.shape[0]
    assert depth == 1, "single-layer step"
    batch, seq, _ = x_prompt.shape
    dec_batch, dec_seq, _ = x_sample.shape
    assert seq % PROMPT_TILE == 0 and dec_batch % SAMPLE_STREAMS == 0

    def layer0(a):
        return a.reshape(a.shape[1:])

    w_out_b, w_up_b, w_down_b = (layer0(w).astype(BF16) for w in (w_out, w_up, w_down))
    norm_mix, norm_ffn, norm_final, gn_w = norm_mix_w, norm_ffn_w, norm_final_w[None], ret_gn_w
    cw = layer0(conv_w)

    rope_m = _rope_table(-N_META, N_META)
    rope_p = _rope_table(0, seq)
    rope_s = _rope_table(PAST_LEN, dec_seq)

    meta_hist, meta_state, w_in_b = pl.pallas_call(
        _meta_kernel,
        out_shape=(jax.ShapeDtypeStruct((CONV_WIDTH - 1, D_CONV), F32),
                   jax.ShapeDtypeStruct((RET_HEADS, RET_HEAD_DIM, RET_HEAD_DIM), F32),
                   jax.ShapeDtypeStruct(w_in.shape[1:], BF16)),
        compiler_params=pltpu.CompilerParams(vmem_limit_bytes=VMEM_LIMIT_BYTES),
        name="meta_state",
    )(meta_tokens, rope_m, norm_mix, layer0(w_in))

    params = (norm_mix, w_in_b, cw, gn_w, w_out_b, norm_ffn, w_up_b, w_down_b, norm_final)
    param_specs = [_const_spec(p.shape) for p in params]

    tiles_per_stream = seq // PROMPT_TILE
    n_tiles = batch * tiles_per_stream

    def cur(t):
        return jnp.minimum(t, n_tiles - 1)

    def prev(t):
        return jnp.maximum(t - 1, 0)

    y_prompt, conv_p, ret_p = pl.pallas_call(
        functools.partial(_prompt_kernel, tiles_per_stream=tiles_per_stream, n_tiles=n_tiles),
        grid=(n_tiles + 1,),
        in_specs=[
            pl.BlockSpec((1, PROMPT_TILE, D_MODEL),
                         lambda t: (cur(t) // tiles_per_stream, cur(t) % tiles_per_stream, 0)),
            pl.BlockSpec((PROMPT_TILE, 2 * RET_HEAD_DIM), lambda t: (cur(t) % tiles_per_stream, 0)),
            _const_spec(meta_hist.shape),
            _const_spec(meta_state.shape),
        ] + param_specs,
        out_specs=(
            pl.BlockSpec((1, PROMPT_TILE, D_MODEL),
                         lambda t: (prev(t) // tiles_per_stream, prev(t) % tiles_per_stream, 0)),
            pl.BlockSpec((1, CONV_WIDTH - 1, D_CONV), lambda t: (cur(t) // tiles_per_stream, 0, 0)),
            pl.BlockSpec((1, RET_HEADS, RET_HEAD_DIM, RET_HEAD_DIM),
                         lambda t: (cur(t) // tiles_per_stream, 0, 0, 0)),
        ),
        out_shape=(
            jax.ShapeDtypeStruct((batch, seq, D_MODEL), F32),
            jax.ShapeDtypeStruct((batch, CONV_WIDTH - 1, D_CONV), F32),
            jax.ShapeDtypeStruct((batch, RET_HEADS, RET_HEAD_DIM, RET_HEAD_DIM), F32),
        ),
        scratch_shapes=[
            pltpu.VMEM((1, PROMPT_TILE + SUBLANES, D_CONV), F32),
            pltpu.VMEM((RET_HEADS, RET_HEAD_DIM, RET_HEAD_DIM), F32),
            pltpu.VMEM((PROMPT_TILE, D_MODEL), F32),
            pltpu.VMEM((RET_HEADS, PROMPT_CHUNK, PROMPT_CHUNK), F32),
            pltpu.VMEM((RET_HEADS, PROMPT_CHUNK, RET_HEAD_DIM), F32),
            pltpu.VMEM((RET_HEADS, PROMPT_CHUNK, RET_HEAD_DIM), F32),
        ],
        compiler_params=pltpu.CompilerParams(
            dimension_semantics=("arbitrary",),
            vmem_limit_bytes=VMEM_LIMIT_BYTES),
        name="prompt_step",
    )(x_prompt, rope_p, meta_hist, meta_state, *params)

    ns = SAMPLE_STREAMS
    hbm_spec = pl.BlockSpec(memory_space=pl.ANY)
    sample_param_specs = [hbm_spec if p is w_up_b or p is w_down_b else spec
                          for p, spec in zip(params, param_specs)]
    y_sample, conv_s, ret_s = pl.pallas_call(
        _sample_kernel,
        grid=(dec_batch // ns,),
        in_specs=[
            pl.BlockSpec((ns, dec_seq, D_MODEL), lambda b: (b, 0, 0)),
            _const_spec(rope_s.shape),
            pl.BlockSpec((ns, CONV_WIDTH - 1, D_CONV), lambda b: (b, 0, 0)),
            pl.BlockSpec((ns, RET_HEADS, RET_HEAD_DIM, RET_HEAD_DIM), lambda b: (b, 0, 0, 0)),
        ] + sample_param_specs,
        out_specs=(
            pl.BlockSpec((ns, dec_seq, D_MODEL), lambda b: (b, 0, 0)),
            pl.BlockSpec((ns, CONV_WIDTH - 1, D_CONV), lambda b: (b, 0, 0)),
            pl.BlockSpec((ns, RET_HEADS, RET_HEAD_DIM, RET_HEAD_DIM), lambda b: (b, 0, 0, 0)),
        ),
        out_shape=(
            jax.ShapeDtypeStruct((dec_batch, dec_seq, D_MODEL), F32),
            jax.ShapeDtypeStruct((dec_batch, CONV_WIDTH - 1, D_CONV), F32),
            jax.ShapeDtypeStruct((dec_batch, RET_HEADS, RET_HEAD_DIM, RET_HEAD_DIM), F32),
        ),
        scratch_shapes=[pltpu.VMEM((ns, dec_seq + SUBLANES, D_CONV), F32),
                        pltpu.VMEM(w_up_b.shape, BF16),
                        pltpu.VMEM(w_down_b.shape, BF16),
                        pltpu.SemaphoreType.DMA((2,))],
        compiler_params=pltpu.CompilerParams(
            dimension_semantics=("arbitrary",),
            vmem_limit_bytes=VMEM_LIMIT_BYTES),
        name="sample_step",
    )(x_sample, rope_s, layer0(state_conv), layer0(state_ret), *params)

    return (y_prompt, y_sample, conv_p[None], ret_p[None], conv_s[None], ret_s[None])
```
